```python
import math
import jax, jax.numpy as jnp
from jax import lax
import numpy as np

D_MODEL = 1024
BATCH = 2
SEQ = 8192
DEPTH = 2

N_HEADS = 16
HEAD_DIM = 64
N_KV_GROUPS = 4
HEADS_PER_GROUP = N_HEADS // N_KV_GROUPS
D_FF = 4 * D_MODEL
N_A_LAYERS = DEPTH // 2
N_B_LAYERS = DEPTH - N_A_LAYERS
CMP_BLOCK = 32
CMP_STRIDE = 16
CMP_HIDDEN = 4 * HEAD_DIM
SEL_BLOCK = 64
SEL_TOPN = 16
WINDOW = 512
NSA_Q_CHUNK = 64
MOBA_BLOCK = 256
MOBA_TOPK = 3
MOBA_Q_CHUNK = 32
REL_BUCKETS = 32
REL_MAX_DIST = 4096
RMS_EPS = 1e-6
NEG = -1e30
BIG = 1e9
NSA_IN_COLS = N_HEADS * HEAD_DIM + 6 * N_KV_GROUPS * HEAD_DIM + 3 * N_HEADS

kernel_name = "nsa_moba_yoco_hybrid"


def rms_norm(x, g):
    xf = x.astype(jnp.float32)
    y = xf * lax.rsqrt(jnp.mean(xf * xf, axis=-1, keepdims=True) + RMS_EPS)
    return (y * g.astype(jnp.float32)).astype(x.dtype)


def rel_bucket(dist):
    n = jnp.maximum(dist, 0)
    max_exact = REL_BUCKETS // 2
    nf = jnp.maximum(n, max_exact).astype(jnp.float32)
    large = max_exact + (jnp.log(nf / max_exact) / math.log(REL_MAX_DIST / max_exact)
                         * (REL_BUCKETS - max_exact)).astype(jnp.int32)
    large = jnp.minimum(large, REL_BUCKETS - 1)
    return jnp.where(n < max_exact, n, large)


def masked_softmax(logits, mask):
    p = jax.nn.softmax(jnp.where(mask, logits, NEG), axis=-1)
    return jnp.where(mask, p, 0.0)


def squared_relu_mlp(h, w_up, w_down):
    a = jax.nn.relu(h @ w_up)
    return (a * a) @ w_down


def nsa_mixer(h, w_in, q_gain, k_gain, cmp_pos, cmp_w1, cmp_w2, w_out, rel_table):
    B, S, _ = h.shape
    H, G, R, Dh = N_HEADS, N_KV_GROUPS, HEADS_PER_GROUP, HEAD_DIM
    scale = Dh ** -0.5
    proj = h @ w_in
    q = rms_norm(proj[..., :H * Dh].reshape(B, S, G, R, Dh), q_gain)
    kv = proj[..., H * Dh:H * Dh + 6 * G * Dh].reshape(B, S, 6, G, Dh)
    gates = jax.nn.sigmoid(proj[..., H * Dh + 6 * G * Dh:].astype(jnp.float32)).reshape(B, S, G, R, 3)
    k_cmp, v_cmp, k_sel, v_sel, k_win, v_win = [kv[:, :, i] for i in range(6)]

    n_cmp = (S - CMP_BLOCK) // CMP_STRIDE + 1
    cidx = np.arange(n_cmp)[:, None] * CMP_STRIDE + np.arange(CMP_BLOCK)[None, :]
    cmp_end = jnp.asarray(cidx[:, -1], jnp.int32)

    def compress(x_, pos, w1, w2):
        z = x_[:, cidx] + pos[None, None, :, None, :]
        hid = jax.nn.silu(jnp.einsum('bnlgd,ldf->bngf', z, w1.reshape(CMP_BLOCK, Dh, CMP_HIDDEN)))
        return jnp.einsum('bngf,fd->bngd', hid, w2)

    kc = rms_norm(compress(k_cmp, cmp_pos[0], cmp_w1[0], cmp_w2[0]), k_gain[0])
    vc = compress(v_cmp, cmp_pos[1], cmp_w1[1], cmp_w2[1])

    n_sel_blk = S // SEL_BLOCK
    cs = np.arange(n_cmp)[:, None] * CMP_STRIDE
    ss = np.arange(n_sel_blk)[None, :] * SEL_BLOCK
    overlap = np.clip(np.minimum(cs + CMP_BLOCK, ss + SEL_BLOCK) - np.maximum(cs, ss), 0, None) / CMP_BLOCK
    cmp_to_sel = jnp.asarray(overlap, jnp.float32)
    n_pick = min(SEL_TOPN, n_sel_blk)

    ks_blk = rms_norm(k_sel, k_gain[1]).reshape(B, n_sel_blk, SEL_BLOCK, G, Dh).transpose(0, 3, 1, 2, 4)
    vs_blk = v_sel.reshape(B, n_sel_blk, SEL_BLOCK, G, Dh).transpose(0, 3, 1, 2, 4)
    pad = ((0, 0), (WINDOW, 0), (0, 0), (0, 0))
    kw_pad = jnp.pad(rms_norm(k_win, k_gain[2]), pad)
    vw_pad = jnp.pad(v_win, pad)

    tb = rel_table.astype(jnp.float32).reshape(REL_BUCKETS, G, R)
    b_ix = jnp.arange(B)[:, None, None, None]
    g_ix = jnp.arange(G)[None, None, :, None]
    g_ix6 = jnp.arange(G)[None, None, :, None, None, None]
    r_ix6 = jnp.arange(R)[None, None, None, :, None, None]
    blk_ids = jnp.arange(n_sel_blk)
    Q = NSA_Q_CHUNK

    def chunk(s):
        t = s + jnp.arange(Q)
        qc = lax.dynamic_slice_in_dim(q, s, Q, axis=1)
        gc = lax.dynamic_slice_in_dim(gates, s, Q, axis=1)
        lc = jnp.einsum('bqgrd,bngd->bqgrn', qc, kc).astype(jnp.float32) * scale
        pc = masked_softmax(lc, (cmp_end[None, :] <= t[:, None])[None, :, None, None, :])
        oc = jnp.einsum('bqgrn,bngd->bqgrd', pc.astype(vc.dtype), vc)
        imp = jnp.einsum('bqgrn,nj->bqgj', pc, cmp_to_sel)
        cur = (t // SEL_BLOCK)[:, None]
        valid = blk_ids[None, :] <= cur
        forced = valid & ((blk_ids[None, :] == 0) | (blk_ids[None, :] >= cur - 1))
        score = jnp.where(forced[None, :, None, :], BIG, jnp.where(valid[None, :, None, :], imp, -BIG))
        _, sidx = lax.top_k(score, n_pick)
        kg = ks_blk[b_ix, g_ix, sidx]
        vg = vs_blk[b_ix, g_ix, sidx]
        kpos = sidx[..., None] * SEL_BLOCK + jnp.arange(SEL_BLOCK)
        dist = t[None, :, None, None, None] - kpos
        bias = tb[rel_bucket(dist)[:, :, :, None], g_ix6, r_ix6]
        ls = jnp.einsum('bqgrd,bqgnld->bqgrnl', qc, kg).astype(jnp.float32) * scale + bias
        ms = (dist >= 0)[:, :, :, None].reshape(B, Q, G, 1, -1)
        ps = masked_softmax(ls.reshape(B, Q, G, R, -1), ms).reshape(ls.shape)
        osel = jnp.einsum('bqgrnl,bqgnld->bqgrd', ps.astype(vg.dtype), vg)
        kw = lax.dynamic_slice_in_dim(kw_pad, s, WINDOW + Q, axis=1)
        vw = lax.dynamic_slice_in_dim(vw_pad, s, WINDOW + Q, axis=1)
        kp = s - WINDOW + jnp.arange(WINDOW + Q)
        dw = t[:, None] - kp[None, :]
        mw = (dw >= 0) & (dw < WINDOW) & (kp[None, :] >= 0)
        bw = tb[rel_bucket(dw)].transpose(0, 2, 3, 1)
        lw = jnp.einsum('bqgrd,bkgd->bqgrk', qc, kw).astype(jnp.float32) * scale + bw[None]
        pw = masked_softmax(lw, mw[None, :, None, None, :])
        ow = jnp.einsum('bqgrk,bkgd->bqgrd', pw.astype(vw.dtype), vw)
        o = gc[..., 0:1] * oc + gc[..., 1:2] * osel + gc[..., 2:3] * ow
        return o.reshape(B, Q, H * Dh).astype(h.dtype)

    starts = jnp.arange(S // Q, dtype=jnp.int32) * Q
    out = lax.map(chunk, starts)
    return jnp.moveaxis(out, 0, 1).reshape(B, S, H * Dh) @ w_out


def shared_kv(h, kv_norm, kv_w, k_gain):
    B, S, _ = h.shape
    G, Dh = N_KV_GROUPS, HEAD_DIM
    kv = (rms_norm(h, kv_norm) @ kv_w).reshape(B, S, 2, G, Dh)
    k = rms_norm(kv[:, :, 0], k_gain)
    v = kv[:, :, 1]
    n_blk = -(-S // MOBA_BLOCK)
    pad = ((0, 0), (0, n_blk * MOBA_BLOCK - S), (0, 0), (0, 0))
    k_blk = jnp.pad(k, pad).reshape(B, n_blk, MOBA_BLOCK, G, Dh).transpose(0, 3, 1, 2, 4)
    v_blk = jnp.pad(v, pad).reshape(B, n_blk, MOBA_BLOCK, G, Dh).transpose(0, 3, 1, 2, 4)
    k_mean = jnp.mean(k_blk.astype(jnp.float32), axis=3).astype(k.dtype)
    return k_blk, v_blk, k_mean


def moba_mixer(h, w_q, q_gain, w_out, k_blk, v_blk, k_mean, rel_table):
    B, S, _ = h.shape
    H, G, R, Dh = N_HEADS, N_KV_GROUPS, HEADS_PER_GROUP, HEAD_DIM
    L = MOBA_BLOCK
    scale = Dh ** -0.5
    q = rms_norm((h @ w_q).reshape(B, S, G, R, Dh), q_gain)
    n_blk = k_blk.shape[2]
    n_pick = min(MOBA_TOPK, n_blk)
    tb = rel_table.astype(jnp.float32).reshape(REL_BUCKETS, G, R)
    blk_ids = jnp.arange(n_blk)
    b_ix5 = jnp.arange(B)[:, None, None, None, None]
    g_ix5 = jnp.arange(G)[None, None, :, None, None]
    g_ix6 = jnp.arange(G)[None, None, :, None, None, None]
    r_ix6 = jnp.arange(R)[None, None, None, :, None, None]
    Q = MOBA_Q_CHUNK

    def chunk(s):
        t = s + jnp.arange(Q)
        cblk = s // L
        qc = lax.dynamic_slice_in_dim(q, s, Q, axis=1)
        gate = jnp.einsum('bqgrd,bgnd->bqgrn', qc, k_mean).astype(jnp.float32)
        gate = jnp.where(blk_ids < cblk, gate, NEG)
        _, bidx = lax.top_k(gate, n_pick)
        kg = k_blk[b_ix5, g_ix5, bidx]
        vg = v_blk[b_ix5, g_ix5, bidx]
        kpos = bidx[..., None] * L + jnp.arange(L)
        dist = t[None, :, None, None, None, None] - kpos
        bp = tb[rel_bucket(dist), g_ix6, r_ix6]
        lp = jnp.einsum('bqgrd,bqgrnld->bqgrnl', qc, kg).astype(jnp.float32) * scale + bp
        mp = jnp.broadcast_to((bidx < cblk)[..., None], lp.shape)
        ko = lax.dynamic_index_in_dim(k_blk, cblk, axis=2, keepdims=False)
        vo = lax.dynamic_index_in_dim(v_blk, cblk, axis=2, keepdims=False)
        do = t[:, None] - (cblk * L + jnp.arange(L))[None, :]
        bo = tb[rel_bucket(do)].transpose(0, 2, 3, 1)
        lo = jnp.einsum('bqgrd,bgld->bqgrl', qc, ko).astype(jnp.float32) * scale + bo[None]
        mo = jnp.broadcast_to((do >= 0)[None, :, None, None, :], lo.shape)
        logits = jnp.concatenate([lp.reshape(B, Q, G, R, -1), lo], axis=-1)
        mask = jnp.concatenate([mp.reshape(B, Q, G, R, -1), mo], axis=-1)
        p = masked_softmax(logits, mask)
        npast = n_pick * L
        pp = p[..., :npast].reshape(lp.shape)
        po = p[..., npast:]
        o = (jnp.einsum('bqgrnl,bqgrnld->bqgrd', pp.astype(vg.dtype), vg)
             + jnp.einsum('bqgrl,bgld->bqgrd', po.astype(vo.dtype), vo))
        return o.reshape(B, Q, H * Dh).astype(h.dtype)

    starts = jnp.arange(S // Q, dtype=jnp.int32) * Q
    out = lax.map(chunk, starts)
    return jnp.moveaxis(out, 0, 1).reshape(B, S, H * Dh) @ w_out


def setup_inputs(seed: int = 0) -> dict:
    key = jax.random.key(seed)
    ks = jax.random.split(key, 20)
    f32 = jnp.float32
    D, H, G, Dh = D_MODEL, N_HEADS, N_KV_GROUPS, HEAD_DIM

    def nrm(k, shape, scale):
        return jax.random.normal(k, shape, f32) * scale

    return {
        "x": nrm(ks[0], (BATCH, SEQ, D), 1.0),
        "norm_mix": 1.0 + nrm(ks[1], (DEPTH, D), 0.1),
        "norm_mlp": 1.0 + nrm(ks[2], (DEPTH, D), 0.1),
        "nsa_w_in": nrm(ks[3], (N_A_LAYERS, D, NSA_IN_COLS), D ** -0.5),
        "nsa_q_gain": 1.0 + nrm(ks[4], (N_A_LAYERS, Dh), 0.1),
        "nsa_k_gain": 1.0 + nrm(ks[5], (N_A_LAYERS, 3, Dh), 0.1),
        "nsa_cmp_pos": nrm(ks[6], (N_A_LAYERS, 2, CMP_BLOCK, Dh), 0.1),
        "nsa_cmp_w1": nrm(ks[7], (N_A_LAYERS, 2, CMP_BLOCK * Dh, CMP_HIDDEN), (CMP_BLOCK * Dh) ** -0.5),
        "nsa_cmp_w2": nrm(ks[8], (N_A_LAYERS, 2, CMP_HIDDEN, Dh), CMP_HIDDEN ** -0.5),
        "nsa_w_out": nrm(ks[9], (N_A_LAYERS, H * Dh, D), (H * Dh) ** -0.5),
        "kv_norm": 1.0 + nrm(ks[10], (D,), 0.1),
        "kv_w": nrm(ks[11], (D, 2 * G * Dh), D ** -0.5),
        "kv_k_gain": 1.0 + nrm(ks[12], (Dh,), 0.1),
        "moba_w_q": nrm(ks[13], (N_B_LAYERS, D, H * Dh), D ** -0.5),
        "moba_q_gain": 1.0 + nrm(ks[14], (N_B_LAYERS, Dh), 0.1),
        "moba_w_out": nrm(ks[15], (N_B_LAYERS, H * Dh, D), (H * Dh) ** -0.5),
        "rel_table": nrm(ks[16], (REL_BUCKETS, H), 0.2),
        "mlp_w_up": nrm(ks[17], (DEPTH, D, D_FF), D ** -0.5),
        "mlp_w_down": nrm(ks[18], (DEPTH, D_FF, D), D_FF ** -0.5),
    }


def reference(x, norm_mix, norm_mlp, nsa_w_in, nsa_q_gain, nsa_k_gain, nsa_cmp_pos, nsa_cmp_w1,
              nsa_cmp_w2, nsa_w_out, kv_norm, kv_w, kv_k_gain, moba_w_q, moba_q_gain, moba_w_out,
              rel_table, mlp_w_up, mlp_w_down):
    h = x
    for layer in range(DEPTH):
        hn = rms_norm(h, norm_mix[layer])
        if layer < N_A_LAYERS:
            i = layer
            h = h + nsa_mixer(hn, nsa_w_in[i], nsa_q_gain[i], nsa_k_gain[i], nsa_cmp_pos[i],
                              nsa_cmp_w1[i], nsa_cmp_w2[i], nsa_w_out[i], rel_table)
        else:
            if layer == N_A_LAYERS:
                k_blk, v_blk, k_mean = shared_kv(h, kv_norm, kv_w, kv_k_gain)
            j = layer - N_A_LAYERS
            h = h + moba_mixer(hn, moba_w_q[j], moba_q_gain[j], moba_w_out[j],
                               k_blk, v_blk, k_mean, rel_table)
        h = h + squared_relu_mlp(rms_norm(h, norm_mlp[layer]), mlp_w_up[layer], mlp_w_down[layer])
    return h
```

```python
import functools
import math

import jax
import jax.numpy as jnp
from jax import lax
from jax.experimental import pallas as pl
from jax.experimental.pallas import tpu as pltpu

F32 = jnp.float32
BF16 = jnp.bfloat16

N_HEADS = 16
HEAD_DIM = 64
N_GROUPS = 4
R = N_HEADS // N_GROUPS
LANES = 128
CMP_BLOCK = 32
CMP_STRIDE = 16
CMP_HIDDEN = 4 * HEAD_DIM
SEL_BLOCK = 64
SEL_TOPN = 16
WINDOW = 512
MOBA_BLOCK = 256
MOBA_TOPK = 3
REL_BUCKETS = 32
REL_MAX_DIST = 4096
RMS_EPS = 1e-6
NEG = -1e30
BIG = 1e9

TQ = 128
TK = 512
ROWS = R * TQ
FAR_DIST = 2897
DELTA_MAX = -(-(FAR_DIST + TK - 1) // LANES) * LANES
STRIP_W = -(-(DELTA_MAX + TK) // 256) * 256
WIN_KEYS = WINDOW + TQ
WIN_STRIP_W = -(-(WINDOW + WIN_KEYS) // 256) * 256
VMEM_LIMIT = 56 * 1024 * 1024


def _cparams(n_axes):
    return pltpu.CompilerParams(dimension_semantics=("arbitrary",) * n_axes,
                                vmem_limit_bytes=VMEM_LIMIT)


def _dot(a, b):
    return jnp.dot(a, b, preferred_element_type=F32)


def _dot_nt(a, b):
    return lax.dot_general(a, b, (((1,), (1,)), ((), ())), preferred_element_type=F32)


def _split_dot(x, w, parts):
    acc = None
    rem = x
    for _ in range(parts):
        hi = rem.astype(BF16)
        term = _dot(hi, w)
        acc = term if acc is None else acc + term
        rem = rem - hi.astype(F32)
    return acc


def _half_masks(shape):
    lane = lax.broadcasted_iota(jnp.int32, shape, len(shape) - 1)
    return lane < HEAD_DIM, lane >= HEAD_DIM


def _strip_kernel(tab_ref, o_ref, *, c0, dlimit, cw):
    g = pl.program_id(0)
    r = pl.program_id(1)
    j = pl.program_id(2)
    h = g * R + r
    shape = (TQ, cw)
    i = lax.broadcasted_iota(jnp.int32, shape, 0)
    c = lax.broadcasted_iota(jnp.int32, shape, 1) + j * cw
    d = i - c + c0
    n = jnp.maximum(d, 0)
    max_exact = REL_BUCKETS // 2
    nf = jnp.maximum(n, max_exact).astype(F32)
    large = max_exact + (jnp.log(nf / max_exact) / math.log(REL_MAX_DIST / max_exact)
                         * (REL_BUCKETS - max_exact)).astype(jnp.int32)
    large = jnp.minimum(large, REL_BUCKETS - 1)
    bucket = jnp.where(n < max_exact, n, large)
    val = jnp.zeros(shape, F32)
    for b in range(REL_BUCKETS):
        val = jnp.where(bucket == b, tab_ref[b, h], val)
    ok = jnp.where(d >= 0, jnp.where(d < dlimit, 1, 0), 0)
    o_ref[...] = jnp.where(ok > 0, val, NEG)


def _bias_strip(rel_table, width, c0, dlimit):
    cw = 256
    return pl.pallas_call(
        functools.partial(_strip_kernel, c0=c0, dlimit=dlimit, cw=cw),
        grid=(N_GROUPS, R, width // cw),
        in_specs=[pl.BlockSpec(memory_space=pltpu.SMEM)],
        out_specs=pl.BlockSpec((None, TQ, cw), lambda g, r, j: (g, r, j)),
        out_shape=jax.ShapeDtypeStruct((N_GROUPS, ROWS, width), F32),
        compiler_params=_cparams(3),
    )(rel_table.astype(F32))


def _head_norm(y, bd_ref, gain):
    outs = []
    for c in range(y.shape[1] // LANES):
        yc = y[:, c * LANES:(c + 1) * LANES]
        ss = _split_dot(yc * yc, bd_ref[...], 2)
        outs.append(yc * lax.rsqrt(ss * (1.0 / HEAD_DIM) + RMS_EPS))
    return jnp.concatenate(outs, axis=1) * gain


def _store_heads(o_ref, y, n_heads):
    lo, hi = _half_masks((y.shape[0], LANES))
    for h in range(n_heads):
        pair = y[:, (h // 2) * LANES:(h // 2 + 1) * LANES]
        o_ref[h] = jnp.where(lo if h % 2 == 0 else hi, pair, 0.0).astype(o_ref.dtype)


def _store_groups_dup(o_ref, y):
    lo, hi = _half_masks((y.shape[0], LANES))
    for g in range(N_GROUPS):
        pair = y[:, (g // 2) * LANES:(g // 2 + 1) * LANES]
        m = jnp.where(lo if g % 2 == 0 else hi, pair, 0.0)
        o_ref[g] = (m + pltpu.roll(m, HEAD_DIM, 1)).astype(o_ref.dtype)


def _nsa_proj_kernel(x_ref, gn_ref, w_ref, bd_ref, qg_ref, kg_ref,
                     q_ref, cmp_ref, ksel_ref, vsel_ref, kwin_ref, vwin_ref, gate_ref):
    x = x_ref[...]
    ms = jnp.mean(x * x, axis=-1, keepdims=True)
    xn = (x * lax.rsqrt(ms + RMS_EPS) * gn_ref[...]).astype(BF16)
    d = N_HEADS * HEAD_DIM
    gd = N_GROUPS * HEAD_DIM
    yq = _dot(xn, w_ref[:, 0:d])
    _store_heads(q_ref, _head_norm(yq, bd_ref, qg_ref[...]), N_HEADS)
    ycmp = _dot(xn, w_ref[:, d:d + 2 * gd])
    for c in range(2 * gd // LANES):
        cmp_ref[c] = ycmp[:, c * LANES:(c + 1) * LANES]
    ysel = _dot(xn, w_ref[:, d + 2 * gd:d + 4 * gd])
    _store_groups_dup(ksel_ref, _head_norm(ysel[:, :gd], bd_ref, kg_ref[0:1, :]))
    _store_groups_dup(vsel_ref, ysel[:, gd:])
    ywin = _dot(xn, w_ref[:, d + 4 * gd:d + 6 * gd])
    _store_groups_dup(kwin_ref, _head_norm(ywin[:, :gd], bd_ref, kg_ref[1:2, :]))
    _store_groups_dup(vwin_ref, ywin[:, gd:])
    yg = _dot(xn, w_ref[:, d + 6 * gd:])
    sg = 1.0 / (1.0 + jnp.exp(-yg))
    for g in range(N_GROUPS):
        gate_ref[g] = sg[:, g * LANES:(g + 1) * LANES]


def _nsa_project(x2, gain, w_in, q_gain, k_gain, B, S):
    D = x2.shape[1]
    d = N_HEADS * HEAD_DIM
    gd = N_GROUPS * HEAD_DIM
    wg = w_in[:, d + 6 * gd:].reshape(D, N_GROUPS, 3 * R)
    wg = jnp.pad(wg, ((0, 0), (0, 0), (0, LANES - 3 * R))).reshape(D, N_GROUPS * LANES)
    w = jnp.concatenate([w_in[:, :d + 6 * gd], wg], axis=1).astype(BF16)
    ncol = w.shape[1]
    bd = jnp.kron(jnp.eye(2, dtype=F32), jnp.ones((HEAD_DIM, HEAD_DIM), F32)).astype(BF16)
    qg = jnp.tile(q_gain.astype(F32) * HEAD_DIM ** -0.5, N_HEADS)[None, :]
    kg = jnp.stack([jnp.tile(k_gain[1].astype(F32), N_GROUPS), jnp.tile(k_gain[2].astype(F32), N_GROUPS)])
    tm = 256
    nt = S // tm
    grp = lambda dt: jax.ShapeDtypeStruct((B, N_GROUPS, S, LANES), dt)
    grp_spec = pl.BlockSpec((None, N_GROUPS, tm, LANES), lambda i: (i // nt, 0, i % nt, 0))
    return pl.pallas_call(
        _nsa_proj_kernel,
        grid=(B * nt,),
        in_specs=[
            pl.BlockSpec((tm, D), lambda i: (i, 0)),
            pl.BlockSpec((1, D), lambda i: (0, 0)),
            pl.BlockSpec((D, ncol), lambda i: (0, 0)),
            pl.BlockSpec((LANES, LANES), lambda i: (0, 0)),
            pl.BlockSpec((1, d), lambda i: (0, 0)),
            pl.BlockSpec((2, gd), lambda i: (0, 0)),
        ],
        out_specs=[
            pl.BlockSpec((None, N_HEADS, tm, LANES), lambda i: (i // nt, 0, i % nt, 0)),
            pl.BlockSpec((2 * gd // LANES, tm, LANES), lambda i: (0, i, 0)),
            grp_spec, grp_spec, grp_spec, grp_spec, grp_spec,
        ],
        out_shape=[
            jax.ShapeDtypeStruct((B, N_HEADS, S, LANES), BF16),
            jax.ShapeDtypeStruct((2 * gd // LANES, B * S, LANES), F32),
            grp(BF16), grp(BF16), grp(BF16), grp(BF16), grp(F32),
        ],
        compiler_params=_cparams(1),
    )(x2, gain[None, :].astype(F32), w, bd, qg, kg)


def _moba_proj_kernel(x_ref, gq_ref, gkv_ref, wq_ref, wkv_ref, bd_ref, qg_ref, kg_ref,
                      q_ref, k_ref, v_ref, kmean_ref):
    x = x_ref[...]
    ms = jnp.mean(x * x, axis=-1, keepdims=True)
    xr = x * lax.rsqrt(ms + RMS_EPS)
    gd = N_GROUPS * HEAD_DIM
    yq = _dot((xr * gq_ref[...]).astype(BF16), wq_ref[...])
    _store_heads(q_ref, _head_norm(yq, bd_ref, qg_ref[...]), N_HEADS)
    ykv = _dot((xr * gkv_ref[...]).astype(BF16), wkv_ref[...])
    kn = _head_norm(ykv[:, :gd], bd_ref, kg_ref[...])
    _store_groups_dup(k_ref, kn)
    _store_groups_dup(v_ref, ykv[:, gd:])
    kmean_ref[...] = jnp.mean(kn, axis=0, keepdims=True)


def _moba_project(h2, g_mix, g_kv, w_q, kv_w, q_gain, k_gain, B, S):
    D = h2.shape[1]
    d = N_HEADS * HEAD_DIM
    gd = N_GROUPS * HEAD_DIM
    bd = jnp.kron(jnp.eye(2, dtype=F32), jnp.ones((HEAD_DIM, HEAD_DIM), F32)).astype(BF16)
    qg = jnp.tile(q_gain.astype(F32) * HEAD_DIM ** -0.5, N_HEADS)[None, :]
    kg = jnp.tile(k_gain.astype(F32), N_GROUPS)[None, :]
    tm = MOBA_BLOCK
    nt = S // tm
    grp = jax.ShapeDtypeStruct((B, N_GROUPS, S, LANES), BF16)
    grp_spec = pl.BlockSpec((None, N_GROUPS, tm, LANES), lambda i: (i // nt, 0, i % nt, 0))
    return pl.pallas_call(
        _moba_proj_kernel,
        grid=(B * nt,),
        in_specs=[
            pl.BlockSpec((tm, D), lambda i: (i, 0)),
            pl.BlockSpec((1, D), lambda i: (0, 0)),
            pl.BlockSpec((1, D), lambda i: (0, 0)),
            pl.BlockSpec((D, d), lambda i: (0, 0)),
            pl.BlockSpec((D, 2 * gd), lambda i: (0, 0)),
            pl.BlockSpec((LANES, LANES), lambda i: (0, 0)),
            pl.BlockSpec((1, d), lambda i: (0, 0)),
            pl.BlockSpec((1, gd), lambda i: (0, 0)),
        ],
        out_specs=[
            pl.BlockSpec((None, N_HEADS, tm, LANES), lambda i: (i // nt, 0, i % nt, 0)),
            grp_spec, grp_spec,
            pl.BlockSpec((None, 1, gd), lambda i: (i, 0, 0)),
        ],
        out_shape=[
            jax.ShapeDtypeStruct((B, N_HEADS, S, LANES), BF16),
            grp, grp,
            jax.ShapeDtypeStruct((B * nt, 1, gd), F32),
        ],
        compiler_params=_cparams(1),
    )(h2, g_mix[None, :].astype(F32), g_kv[None, :].astype(F32), w_q.astype(BF16), kv_w.astype(BF16),
      bd, qg, kg)


def _compress_kernel(x_ref, pos_ref, w1_ref, w2_ref, kg_ref, o_ref, *, n_chunk):
    t = pl.program_id(1)
    half = CMP_BLOCK // 2
    za = [[], []]
    zb = [[], []]
    for l in range(half):
        for p in range(2):
            xl = x_ref[p, pl.ds(l, n_chunk, stride=CMP_STRIDE), :]
            za[p].append((xl + pos_ref[l:l + 1, p * LANES:(p + 1) * LANES]).astype(BF16))
            zb[p].append((xl + pos_ref[half + l:half + l + 1, p * LANES:(p + 1) * LANES]).astype(BF16))
    za = [jnp.concatenate(z, axis=1) for z in za]
    zb = [jnp.concatenate(z, axis=1) for z in zb]
    for g in range(N_GROUPS):
        p, e = g // 2, g % 2
        first = _dot(za[p], w1_ref[0, e])
        second = _dot(zb[p], w1_ref[1, e])
        hid = first + pltpu.roll(second, n_chunk - 1, 0)
        hid = hid * (1.0 / (1.0 + jnp.exp(-hid)))
        out = _dot(hid.astype(BF16), w2_ref[...])
        normed = out * lax.rsqrt(jnp.mean(out * out, axis=-1, keepdims=True) + RMS_EPS) * kg_ref[...]
        o_ref[g] = jnp.where(t == 0, normed, out).astype(o_ref.dtype)


def _nsa_compress(cmp_raw, cmp_pos, cmp_w1, cmp_w2, k_gain0, B, S):
    gd = N_GROUPS * HEAD_DIM
    half = CMP_BLOCK // 2
    n_chunk = S // CMP_STRIDE
    pos = jnp.tile(cmp_pos.astype(F32), (1, 1, N_GROUPS))
    w1 = cmp_w1.astype(F32).reshape(2, 2, half, HEAD_DIM, CMP_HIDDEN)
    z = jnp.zeros_like(w1)
    w1p = jnp.stack([jnp.concatenate([w1, z], axis=3), jnp.concatenate([z, w1], axis=3)], axis=2)
    w1p = w1p.reshape(2, 2, 2, half * LANES, CMP_HIDDEN).astype(BF16)
    w2 = jnp.concatenate([cmp_w2, cmp_w2], axis=-1).astype(BF16)
    kg = jnp.tile(k_gain0.astype(F32), 2)[None, :]
    return pl.pallas_call(
        functools.partial(_compress_kernel, n_chunk=n_chunk),
        grid=(B, 2),
        in_specs=[
            pl.BlockSpec((2, S, LANES), lambda b, t: (t, b, 0)),
            pl.BlockSpec((None, CMP_BLOCK, gd), lambda b, t: (t, 0, 0)),
            pl.BlockSpec((None, 2, 2, half * LANES, CMP_HIDDEN), lambda b, t: (t, 0, 0, 0, 0)),
            pl.BlockSpec((None, CMP_HIDDEN, LANES), lambda b, t: (t, 0, 0)),
            pl.BlockSpec((1, LANES), lambda b, t: (0, 0)),
        ],
        out_specs=pl.BlockSpec((None, None, N_GROUPS, n_chunk, LANES), lambda b, t: (t, b, 0, 0, 0)),
        out_shape=jax.ShapeDtypeStruct((2, B, N_GROUPS, n_chunk, LANES), BF16),
        compiler_params=_cparams(2),
    )(cmp_raw, pos, w1p, w2, kg)


def _topk_mask(score, k):
    lane = lax.broadcasted_iota(jnp.int32, score.shape, 1).astype(F32)
    sel = jnp.zeros(score.shape, F32)
    s = score
    for _ in range(k):
        m = jnp.max(s, axis=-1, keepdims=True)
        idx = jnp.min(jnp.where(s == m, lane, float(LANES)), axis=-1, keepdims=True)
        pick = lane == idx
        sel = jnp.where(pick, 1.0, sel)
        s = jnp.where(pick, -jnp.inf, s)
    return sel


def _assemble_heads(o, gate, gate_col):
    lo, hi = _half_masks((TQ, LANES))
    cols = []
    for p in range(R // 2):
        parts = []
        for e, msk in ((0, lo), (1, hi)):
            r = 2 * p + e
            o_r = o[r * TQ:(r + 1) * TQ, :]
            if gate is not None:
                c = 3 * r + gate_col
                o_r = o_r * gate[:, c:c + 1]
            parts.append(jnp.where(msk, o_r, 0.0))
        cols.append(parts[0] + parts[1])
    return jnp.concatenate(cols, axis=1)


def _flash_tiles(qa, k_ref, oh_ref, v_ref, strip_ref, q0, n_tiles):
    def body(kt, carry):
        m, l, acc = carry
        k0 = pl.multiple_of(kt * TK, TK)
        ka = jnp.concatenate([oh_ref[pl.ds(k0, TK), :], k_ref[pl.ds(k0, TK), :]], axis=1)
        s = _dot_nt(qa, ka)
        cs = pl.multiple_of(DELTA_MAX - jnp.minimum(q0 - k0, DELTA_MAX), LANES)
        s = s + strip_ref[:, pl.ds(cs, TK)]
        m_new = jnp.maximum(m, jnp.max(s, axis=-1, keepdims=True))
        alpha = jnp.exp(m - m_new)
        p = jnp.exp(s - m_new)
        l = alpha * l + jnp.sum(p, axis=-1, keepdims=True)
        acc = alpha * acc + _dot(p.astype(BF16), v_ref[pl.ds(k0, TK), :])
        return m_new, l, acc

    init = (jnp.full((ROWS, 1), 3 * NEG, F32), jnp.zeros((ROWS, 1), F32), jnp.zeros((ROWS, LANES), F32))
    m, l, acc = lax.fori_loop(0, n_tiles, body, init)
    return acc / l


def _nsa_cmp_kernel(q_ref, kc_ref, vc_ref, c2s_ref, gate_ref, o_ref, mb_ref, *, n_chunk):
    q0 = pl.program_id(2) * TQ
    q = q_ref[...].reshape(ROWS, LANES)
    lc = _dot_nt(q, kc_ref[...])
    row = lax.broadcasted_iota(jnp.int32, (ROWS, n_chunk), 0)
    t = q0 + (row & (TQ - 1))
    n = lax.broadcasted_iota(jnp.int32, (ROWS, n_chunk), 1)
    mask = n * CMP_STRIDE + (CMP_BLOCK - 1) <= t
    lcm = jnp.where(mask, lc, NEG)
    e = jnp.exp(lcm - jnp.max(lcm, axis=-1, keepdims=True))
    pc = jnp.where(mask, e / jnp.sum(e, axis=-1, keepdims=True), 0.0)
    oc = _dot(pc.astype(BF16), vc_ref[...])
    o_ref[...] = _assemble_heads(oc, gate_ref[...], 0)
    psum = pc[0:TQ]
    for r in range(1, R):
        psum = psum + pc[r * TQ:(r + 1) * TQ]
    imp = _split_dot(psum, c2s_ref[...], 3)
    tq_pos = q0 + lax.broadcasted_iota(jnp.int32, (TQ, LANES), 0)
    cur = tq_pos // SEL_BLOCK
    j = lax.broadcasted_iota(jnp.int32, (TQ, LANES), 1)
    valid = j <= cur
    forced_or_imp = jnp.where(j == 0, BIG, jnp.where(j >= cur - 1, BIG, imp))
    score = jnp.where(valid, forced_or_imp, -BIG)
    sel = _topk_mask(score, SEL_TOPN)
    mb_ref[...] = jnp.where(sel > 0, 0.0, NEG).astype(mb_ref.dtype)


def _nsa_cmp(q_hm, kc, vc, gates, B, S):
    n_chunk = S // CMP_STRIDE
    n_sel = S // SEL_BLOCK
    assert n_sel <= LANES and n_chunk % LANES == 0
    cs = jnp.arange(n_chunk)[:, None] * CMP_STRIDE
    ss = jnp.arange(LANES)[None, :] * SEL_BLOCK
    overlap = jnp.clip(jnp.minimum(cs + CMP_BLOCK, ss + SEL_BLOCK) - jnp.maximum(cs, ss), 0, None) / CMP_BLOCK
    n_cmp = (S - CMP_BLOCK) // CMP_STRIDE + 1
    c2s = jnp.where((jnp.arange(n_chunk)[:, None] < n_cmp) & (jnp.arange(LANES)[None, :] < n_sel),
                    overlap, 0.0).astype(BF16)
    nq = S // TQ
    return pl.pallas_call(
        functools.partial(_nsa_cmp_kernel, n_chunk=n_chunk),
        grid=(B, N_GROUPS, nq),
        in_specs=[
            pl.BlockSpec((None, R, TQ, LANES), lambda b, g, i: (b, g, i, 0)),
            pl.BlockSpec((None, None, n_chunk, LANES), lambda b, g, i: (b, g, 0, 0)),
            pl.BlockSpec((None, None, n_chunk, LANES), lambda b, g, i: (b, g, 0, 0)),
            pl.BlockSpec((n_chunk, LANES), lambda b, g, i: (0, 0)),
            pl.BlockSpec((None, None, TQ, LANES), lambda b, g, i: (b, g, i, 0)),
        ],
        out_specs=[
            pl.BlockSpec((None, TQ, R * HEAD_DIM), lambda b, g, i: (b, i, g)),
            pl.BlockSpec((None, None, TQ, LANES), lambda b, g, i: (b, g, i, 0)),
        ],
        out_shape=[
            jax.ShapeDtypeStruct((B, S, N_HEADS * HEAD_DIM), F32),
            jax.ShapeDtypeStruct((B, N_GROUPS, S, LANES), BF16),
        ],
        compiler_params=_cparams(3),
    )(q_hm, kc, vc, c2s, gates)


def _nsa_sel_kernel(q_ref, mb_ref, k_ref, oh_ref, v_ref, strip_ref, gate_ref, o_ref):
    q0 = pl.program_id(2) * TQ
    q = q_ref[...].reshape(ROWS, LANES)
    mb = mb_ref[...]
    qa = jnp.concatenate([jnp.concatenate([mb] * R, axis=0), q], axis=1)
    n_tiles = (q0 + TQ + TK - 1) // TK
    o = _flash_tiles(qa, k_ref, oh_ref, v_ref, strip_ref, q0, n_tiles)
    o_ref[...] = _assemble_heads(o, gate_ref[...], 1)


def _nsa_sel(q_hm, mb, ksel, vsel, strip, gates, B, S):
    onehot = (jnp.arange(S)[:, None] // SEL_BLOCK == jnp.arange(LANES)[None, :]).astype(BF16)
    nq = S // TQ
    return pl.pallas_call(
        _nsa_sel_kernel,
        grid=(B, N_GROUPS, nq),
        in_specs=[
            pl.BlockSpec((None, R, TQ, LANES), lambda b, g, i: (b, g, i, 0)),
            pl.BlockSpec((None, None, TQ, LANES), lambda b, g, i: (b, g, i, 0)),
            pl.BlockSpec((None, None, S, LANES), lambda b, g, i: (b, g, 0, 0)),
            pl.BlockSpec((S, LANES), lambda b, g, i: (0, 0)),
            pl.BlockSpec((None, None, S, LANES), lambda b, g, i: (b, g, 0, 0)),
            pl.BlockSpec((None, ROWS, STRIP_W), lambda b, g, i: (g, 0, 0)),
            pl.BlockSpec((None, None, TQ, LANES), lambda b, g, i: (b, g, i, 0)),
        ],
        out_specs=pl.BlockSpec((None, TQ, R * HEAD_DIM), lambda b, g, i: (b, i, g)),
        out_shape=jax.ShapeDtypeStruct((B, S, N_HEADS * HEAD_DIM), F32),
        compiler_params=_cparams(3),
    )(q_hm, mb, ksel, onehot, vsel, strip, gates)


def _nsa_win_kernel(q_ref, k_ref, v_ref, strip_ref, gate_ref, o_ref):
    q0 = pl.program_id(2) * TQ
    q = q_ref[...].reshape(ROWS, LANES)
    k0 = pl.multiple_of(jnp.maximum(q0 - WINDOW, 0), TQ)
    s = _dot_nt(q, k_ref[pl.ds(k0, WIN_KEYS), :])
    cs = pl.multiple_of(WINDOW - (q0 - k0), LANES)
    s = s + strip_ref[:, pl.ds(cs, WIN_KEYS)]
    e = jnp.exp(s - jnp.max(s, axis=-1, keepdims=True))
    p = e / jnp.sum(e, axis=-1, keepdims=True)
    o = _dot(p.astype(BF16), v_ref[pl.ds(k0, WIN_KEYS), :])
    o_ref[...] = _assemble_heads(o, gate_ref[...], 2)


def _nsa_win(q_hm, kwin, vwin, strip, gates, B, S):
    assert S >= WIN_KEYS
    nq = S // TQ
    return pl.pallas_call(
        _nsa_win_kernel,
        grid=(B, N_GROUPS, nq),
        in_specs=[
            pl.BlockSpec((None, R, TQ, LANES), lambda b, g, i: (b, g, i, 0)),
            pl.BlockSpec((None, None, S, LANES), lambda b, g, i: (b, g, 0, 0)),
            pl.BlockSpec((None, None, S, LANES), lambda b, g, i: (b, g, 0, 0)),
            pl.BlockSpec((None, ROWS, WIN_STRIP_W), lambda b, g, i: (g, 0, 0)),
            pl.BlockSpec((None, None, TQ, LANES), lambda b, g, i: (b, g, i, 0)),
        ],
        out_specs=pl.BlockSpec((None, TQ, R * HEAD_DIM), lambda b, g, i: (b, i, g)),
        out_shape=jax.ShapeDtypeStruct((B, S, N_HEADS * HEAD_DIM), F32),
        compiler_params=_cparams(3),
    )(q_hm, kwin, vwin, strip, gates)


def _moba_kernel(q_ref, kmean_ref, k_ref, oh_ref, v_ref, strip_ref, o_ref):
    q0 = pl.program_id(2) * TQ
    cblk = q0 // MOBA_BLOCK
    q = q_ref[...].reshape(ROWS, LANES)
    gate = _dot_nt(q, kmean_ref[...])
    n = lax.broadcasted_iota(jnp.int32, (ROWS, LANES), 1)
    past = n < cblk
    sel = _topk_mask(jnp.where(past, gate, NEG), MOBA_TOPK)
    mb = jnp.where(n == cblk, 0.0, jnp.where(past, jnp.where(sel > 0, 0.0, NEG), NEG)).astype(BF16)
    qa = jnp.concatenate([mb, q], axis=1)
    n_tiles = (q0 + TQ + TK - 1) // TK
    o = _flash_tiles(qa, k_ref, oh_ref, v_ref, strip_ref, q0, n_tiles)
    o_ref[...] = _assemble_heads(o, None, 0)


def _moba_attn(q_hm, kmean, k, v, strip, B, S):
    n_blk = S // MOBA_BLOCK
    assert n_blk <= LANES
    onehot = (jnp.arange(S)[:, None] // MOBA_BLOCK == jnp.arange(LANES)[None, :]).astype(BF16)
    nq = S // TQ
    return pl.pallas_call(
        _moba_kernel,
        grid=(B, N_GROUPS, nq),
        in_specs=[
            pl.BlockSpec((None, R, TQ, LANES), lambda b, g, i: (b, g, i, 0)),
            pl.BlockSpec((None, None, LANES, LANES), lambda b, g, i: (b, g, 0, 0)),
            pl.BlockSpec((None, None, S, LANES), lambda b, g, i: (b, g, 0, 0)),
            pl.BlockSpec((S, LANES), lambda b, g, i: (0, 0)),
            pl.BlockSpec((None, None, S, LANES), lambda b, g, i: (b, g, 0, 0)),
            pl.BlockSpec((None, ROWS, STRIP_W), lambda b, g, i: (g, 0, 0)),
        ],
        out_specs=pl.BlockSpec((None, TQ, R * HEAD_DIM), lambda b, g, i: (b, i, g)),
        out_shape=jax.ShapeDtypeStruct((B, S, N_HEADS * HEAD_DIM), F32),
        compiler_params=_cparams(3),
    )(q_hm, kmean, k, onehot, v, strip)


def _out_proj_kernel(*refs):
    *o_refs, w_ref, h_ref, out_ref = refs
    o = o_refs[0][...]
    for r in o_refs[1:]:
        o = o + r[...]
    out_ref[...] = h_ref[...] + _dot(o.astype(BF16), w_ref[...])


def _out_proj(parts, w_out, h2):
    n, d = parts[0].shape
    D = w_out.shape[1]
    tm = 512
    return pl.pallas_call(
        _out_proj_kernel,
        grid=(n // tm,),
        in_specs=[pl.BlockSpec((tm, d), lambda i: (i, 0)) for _ in parts] + [
            pl.BlockSpec((d, D), lambda i: (0, 0)),
            pl.BlockSpec((tm, D), lambda i: (i, 0)),
        ],
        out_specs=pl.BlockSpec((tm, D), lambda i: (i, 0)),
        out_shape=jax.ShapeDtypeStruct((n, D), F32),
        compiler_params=_cparams(1),
    )(*parts, w_out.astype(BF16), h2)


def _mlp_kernel(h_ref, g_ref, wu_ref, wd_ref, out_ref, *, ff_chunk):
    h = h_ref[...]
    ms = jnp.mean(h * h, axis=-1, keepdims=True)
    xn = (h * lax.rsqrt(ms + RMS_EPS) * g_ref[...]).astype(BF16)
    acc = h
    for c in range(wu_ref.shape[1] // ff_chunk):
        a = jnp.maximum(_dot(xn, wu_ref[:, c * ff_chunk:(c + 1) * ff_chunk]), 0.0)
        acc = acc + _dot((a * a).astype(BF16), wd_ref[c * ff_chunk:(c + 1) * ff_chunk, :])
    out_ref[...] = acc


def _mlp(h2, gain, w_up, w_down):
    n, D = h2.shape
    F = w_up.shape[1]
    tm = 256
    return pl.pallas_call(
        functools.partial(_mlp_kernel, ff_chunk=1024),
        grid=(n // tm,),
        in_specs=[
            pl.BlockSpec((tm, D), lambda i: (i, 0)),
            pl.BlockSpec((1, D), lambda i: (0, 0)),
            pl.BlockSpec((D, F), lambda i: (0, 0)),
            pl.BlockSpec((F, D), lambda i: (0, 0)),
        ],
        out_specs=pl.BlockSpec((tm, D), lambda i: (i, 0)),
        out_shape=jax.ShapeDtypeStruct((n, D), F32),
        compiler_params=_cparams(1),
    )(h2, gain[None, :].astype(F32), w_up.astype(BF16), w_down.astype(BF16))


def _nsa_layer(h2, B, S, norm_mix, w_in, q_gain, k_gain, cmp_pos, cmp_w1, cmp_w2, w_out, sel_strip, win_strip):
    q_hm, cmp_raw, ksel, vsel, kwin, vwin, gates = _nsa_project(h2, norm_mix, w_in, q_gain, k_gain, B, S)
    kvc = _nsa_compress(cmp_raw, cmp_pos, cmp_w1, cmp_w2, k_gain[0], B, S)
    o_cmp, mb = _nsa_cmp(q_hm, kvc[0], kvc[1], gates, B, S)
    o_sel = _nsa_sel(q_hm, mb, ksel, vsel, sel_strip, gates, B, S)
    o_win = _nsa_win(q_hm, kwin, vwin, win_strip, gates, B, S)
    d = N_HEADS * HEAD_DIM
    return _out_proj([o_cmp.reshape(B * S, d), o_sel.reshape(B * S, d), o_win.reshape(B * S, d)], w_out, h2)


def _shared_kv_and_q(h2, B, S, norm_mix, kv_norm, kv_w, kv_k_gain, w_q, q_gain):
    q_hm, k, v, kmean = _moba_project(h2, norm_mix, kv_norm, w_q, kv_w, q_gain, kv_k_gain, B, S)
    n_blk = S // MOBA_BLOCK
    km = kmean.reshape(B, n_blk, N_GROUPS, HEAD_DIM).transpose(0, 2, 1, 3)
    km = jnp.concatenate([km, km], axis=-1)
    km = jnp.pad(km, ((0, 0), (0, 0), (0, LANES - n_blk), (0, 0))).astype(BF16)
    return q_hm, k, v, km


def kernel(x, norm_mix, norm_mlp, nsa_w_in, nsa_q_gain, nsa_k_gain, nsa_cmp_pos, nsa_cmp_w1, nsa_cmp_w2,
           nsa_w_out, kv_norm, kv_w, kv_k_gain, moba_w_q, moba_q_gain, moba_w_out, rel_table, mlp_w_up,
           mlp_w_down):
    B, S, D = x.shape
    depth = norm_mix.shape[0]
    n_a = nsa_w_in.shape[0]
    assert S % TK == 0 and S % MOBA_BLOCK == 0
    sel_strip = _bias_strip(rel_table, STRIP_W, DELTA_MAX, 1 << 30)
    win_strip = _bias_strip(rel_table, WIN_STRIP_W, WINDOW, WINDOW)
    h2 = x.reshape(B * S, D)
    d = N_HEADS * HEAD_DIM
    shared = None
    for layer in range(depth):
        if layer < n_a:
            i = layer
            h2 = _nsa_layer(h2, B, S, norm_mix[layer], nsa_w_in[i], nsa_q_gain[i], nsa_k_gain[i],
                            nsa_cmp_pos[i], nsa_cmp_w1[i], nsa_cmp_w2[i], nsa_w_out[i], sel_strip, win_strip)
        else:
            j = layer - n_a
            if shared is None:
                q_hm, k, v, km = _shared_kv_and_q(h2, B, S, norm_mix[layer], kv_norm, kv_w, kv_k_gain,
                                                  moba_w_q[j], moba_q_gain[j])
                shared = (k, v, km)
            else:
                q_hm = _moba_project(h2, norm_mix[layer], kv_norm, moba_w_q[j], kv_w, moba_q_gain[j],
                                     kv_k_gain, B, S)[0]
            k, v, km = shared
            o = _moba_attn(q_hm, km, k, v, sel_strip, B, S)
            h2 = _out_proj([o.reshape(B * S, d)], moba_w_out[j], h2)
        h2 = _mlp(h2, norm_mlp[layer], mlp_w_up[layer], mlp_w_down[layer])
    return h2.reshape(B, S, D)
```

```python
import functools
import math

import jax
import jax.numpy as jnp
from jax import lax
from jax.experimental import pallas as pl
from jax.experimental.pallas import tpu as pltpu

F32 = jnp.float32
BF16 = jnp.bfloat16

N_HEADS = 16
HEAD_DIM = 64
N_GROUPS = 4
R = N_HEADS // N_GROUPS
LANES = 128
CMP_BLOCK = 32
CMP_STRIDE = 16
CMP_HIDDEN = 4 * HEAD_DIM
SEL_BLOCK = 64
SEL_TOPN = 16
WINDOW = 512
MOBA_BLOCK = 256
MOBA_TOPK = 3
REL_BUCKETS = 32
REL_MAX_DIST = 4096
RMS_EPS = 1e-6
NEG = -1e30
BIG = 1e9
LOG2E = math.log2(math.e)
GATE_ROWS = 16

TQ = 128
TK = 512
ROWS = R * TQ
FAR_DIST = 2897
DELTA_MAX = -(-(FAR_DIST + TK - 1) // LANES) * LANES
STRIP_W = -(-(DELTA_MAX + TK) // 256) * 256
WIN_KEYS = WINDOW + TQ
WIN_STRIP_W = -(-(WINDOW + WIN_KEYS) // 256) * 256
VMEM_LIMIT = 56 * 1024 * 1024


def _cparams(n_axes):
    return pltpu.CompilerParams(dimension_semantics=("arbitrary",) * n_axes,
                                vmem_limit_bytes=VMEM_LIMIT)


def _dot(a, b):
    return jnp.dot(a, b, preferred_element_type=F32)


def _split3(x):
    parts = []
    rem = x
    for _ in range(3):
        hi = rem.astype(BF16)
        parts.append(hi)
        rem = rem - hi.astype(F32)
    return parts


def _strip_kernel(tab_ref, o_ref, *, c0, dlimit, ch):
    g = pl.program_id(0)
    j = pl.program_id(1)
    r = pl.program_id(2)
    h = g * R + r
    shape = (ch, TQ)
    c = lax.broadcasted_iota(jnp.int32, shape, 0) + j * ch
    i = lax.broadcasted_iota(jnp.int32, shape, 1)
    d = i - c + c0
    n = jnp.maximum(d, 0)
    max_exact = REL_BUCKETS // 2
    nf = jnp.maximum(n, max_exact).astype(F32)
    large = max_exact + (jnp.log(nf / max_exact) / math.log(REL_MAX_DIST / max_exact)
                         * (REL_BUCKETS - max_exact)).astype(jnp.int32)
    large = jnp.minimum(large, REL_BUCKETS - 1)
    bucket = jnp.where(n < max_exact, n, large)
    val = jnp.zeros(shape, F32)
    for b in range(REL_BUCKETS):
        val = jnp.where(bucket == b, tab_ref[b, h], val)
    ok = jnp.where(d >= 0, jnp.where(d < dlimit, 1, 0), 0)
    o_ref[...] = jnp.where(ok > 0, val * LOG2E, NEG)


def _bias_strip(rel_table, width, c0, dlimit):
    ch = 256
    return pl.pallas_call(
        functools.partial(_strip_kernel, c0=c0, dlimit=dlimit, ch=ch),
        grid=(N_GROUPS, width // ch, R),
        in_specs=[pl.BlockSpec(memory_space=pltpu.SMEM)],
        out_specs=pl.BlockSpec((None, ch, TQ), lambda g, j, r: (g, j, r)),
        out_shape=jax.ShapeDtypeStruct((N_GROUPS, width, ROWS), F32),
        compiler_params=_cparams(3),
    )(rel_table.astype(F32))


def _head_norm(y, bd_ref, gain):
    outs = []
    for c in range(y.shape[1] // LANES):
        yc = y[:, c * LANES:(c + 1) * LANES]
        y2 = yc * yc
        hi = y2.astype(BF16)
        lo = (y2 - hi.astype(F32)).astype(BF16)
        ss = _dot(hi, bd_ref[...]) + _dot(lo, bd_ref[...])
        outs.append(yc * lax.rsqrt(ss * (1.0 / HEAD_DIM) + RMS_EPS))
    return jnp.concatenate(outs, axis=1) * gain


def _store_heads_t(q_ref, y):
    yt = y.T
    row = lax.broadcasted_iota(jnp.int32, (LANES, y.shape[0]), 0)
    for h in range(N_HEADS):
        pair = yt[(h // 2) * LANES:(h // 2 + 1) * LANES, :]
        keep = row < HEAD_DIM if h % 2 == 0 else row >= HEAD_DIM
        q_ref[h] = jnp.where(keep, pair, 0.0).astype(q_ref.dtype)


def _store_groups_dup(o_ref, y):
    lane = lax.broadcasted_iota(jnp.int32, (y.shape[0], LANES), 1)
    for g in range(N_GROUPS):
        pair = y[:, (g // 2) * LANES:(g // 2 + 1) * LANES]
        keep = lane < HEAD_DIM if g % 2 == 0 else lane >= HEAD_DIM
        m = jnp.where(keep, pair, 0.0)
        o_ref[g] = (m + pltpu.roll(m, HEAD_DIM, 1)).astype(o_ref.dtype)


def _store_groups_t(o_ref, y):
    yt = y.T
    for g in range(N_GROUPS):
        o_ref[g] = yt[g * HEAD_DIM:(g + 1) * HEAD_DIM, :].astype(o_ref.dtype)


def _nsa_proj_kernel(x_ref, gn_ref, w_ref, bd_ref, qg_ref, kg_ref,
                     q_ref, cmp_ref, ksel_ref, vsel_ref, kwin_ref, vwin_ref, gate_ref):
    x = x_ref[...]
    ms = jnp.mean(x * x, axis=-1, keepdims=True)
    xn = (x * lax.rsqrt(ms + RMS_EPS) * gn_ref[...]).astype(BF16)
    d = N_HEADS * HEAD_DIM
    gd = N_GROUPS * HEAD_DIM
    yq = _dot(xn, w_ref[:, 0:d])
    _store_heads_t(q_ref, _head_norm(yq, bd_ref, qg_ref[...]))
    ycmp = _dot(xn, w_ref[:, d:d + 2 * gd])
    for c in range(2 * gd // LANES):
        cmp_ref[c] = ycmp[:, c * LANES:(c + 1) * LANES]
    ysel = _dot(xn, w_ref[:, d + 2 * gd:d + 4 * gd])
    _store_groups_dup(ksel_ref, _head_norm(ysel[:, :gd], bd_ref, kg_ref[0:1, :]))
    _store_groups_t(vsel_ref, ysel[:, gd:])
    ywin = _dot(xn, w_ref[:, d + 4 * gd:d + 6 * gd])
    _store_groups_dup(kwin_ref, _head_norm(ywin[:, :gd], bd_ref, kg_ref[1:2, :]))
    _store_groups_t(vwin_ref, ywin[:, gd:])
    yg = _dot(xn, w_ref[:, d + 6 * gd:])
    sgt = (1.0 / (1.0 + jnp.exp(-yg))).T
    for g in range(N_GROUPS):
        gate_ref[g] = sgt[g * LANES:g * LANES + GATE_ROWS, :]


def _qk_side_inputs(q_gain):
    bd = jnp.kron(jnp.eye(2, dtype=F32), jnp.ones((HEAD_DIM, HEAD_DIM), F32)).astype(BF16)
    qg = jnp.tile(q_gain.astype(F32) * (HEAD_DIM ** -0.5 * LOG2E), N_HEADS)[None, :]
    return bd, qg


def _nsa_project(x2, gain, w_in, q_gain, k_gain, B, S):
    D = x2.shape[1]
    d = N_HEADS * HEAD_DIM
    gd = N_GROUPS * HEAD_DIM
    wg = w_in[:, d + 6 * gd:].reshape(D, N_GROUPS, 3 * R)
    wg = jnp.pad(wg, ((0, 0), (0, 0), (0, LANES - 3 * R))).reshape(D, N_GROUPS * LANES)
    w = jnp.concatenate([w_in[:, :d + 6 * gd], wg], axis=1).astype(BF16)
    ncol = w.shape[1]
    bd, qg = _qk_side_inputs(q_gain)
    kg = jnp.stack([jnp.tile(k_gain[1].astype(F32), N_GROUPS), jnp.tile(k_gain[2].astype(F32), N_GROUPS)])
    tm = 256
    nt = S // tm
    kspec = pl.BlockSpec((None, N_GROUPS, tm, LANES), lambda i: (i // nt, 0, i % nt, 0))
    vspec = pl.BlockSpec((None, N_GROUPS, HEAD_DIM, tm), lambda i: (i // nt, 0, 0, i % nt))
    kshape = jax.ShapeDtypeStruct((B, N_GROUPS, S, LANES), BF16)
    vshape = jax.ShapeDtypeStruct((B, N_GROUPS, HEAD_DIM, S), BF16)
    return pl.pallas_call(
        _nsa_proj_kernel,
        grid=(B * nt,),
        in_specs=[
            pl.BlockSpec((tm, D), lambda i: (i, 0)),
            pl.BlockSpec((1, D), lambda i: (0, 0)),
            pl.BlockSpec((D, ncol), lambda i: (0, 0)),
            pl.BlockSpec((LANES, LANES), lambda i: (0, 0)),
            pl.BlockSpec((1, d), lambda i: (0, 0)),
            pl.BlockSpec((2, gd), lambda i: (0, 0)),
        ],
        out_specs=[
            pl.BlockSpec((None, N_HEADS, LANES, tm), lambda i: (i // nt, 0, 0, i % nt)),
            pl.BlockSpec((2 * gd // LANES, tm, LANES), lambda i: (0, i, 0)),
            kspec, vspec, kspec, vspec,
            pl.BlockSpec((None, N_GROUPS, GATE_ROWS, tm), lambda i: (i // nt, 0, 0, i % nt)),
        ],
        out_shape=[
            jax.ShapeDtypeStruct((B, N_HEADS, LANES, S), BF16),
            jax.ShapeDtypeStruct((2 * gd // LANES, B * S, LANES), F32),
            kshape, vshape, kshape, vshape,
            jax.ShapeDtypeStruct((B, N_GROUPS, GATE_ROWS, S), F32),
        ],
        compiler_params=_cparams(1),
    )(x2, gain[None, :].astype(F32), w, bd, qg, kg)


def _moba_proj_kernel(x_ref, gq_ref, gkv_ref, wq_ref, wkv_ref, bd_ref, qg_ref, kg_ref,
                      q_ref, k_ref, v_ref, kmean_ref):
    x = x_ref[...]
    ms = jnp.mean(x * x, axis=-1, keepdims=True)
    xr = x * lax.rsqrt(ms + RMS_EPS)
    gd = N_GROUPS * HEAD_DIM
    yq = _dot((xr * gq_ref[...]).astype(BF16), wq_ref[...])
    _store_heads_t(q_ref, _head_norm(yq, bd_ref, qg_ref[...]))
    ykv = _dot((xr * gkv_ref[...]).astype(BF16), wkv_ref[...])
    kn = _head_norm(ykv[:, :gd], bd_ref, kg_ref[...])
    _store_groups_dup(k_ref, kn)
    _store_groups_t(v_ref, ykv[:, gd:])
    kmean_ref[...] = jnp.mean(kn, axis=0, keepdims=True)


def _moba_project(h2, g_mix, g_kv, w_q, kv_w, q_gain, k_gain, B, S):
    D = h2.shape[1]
    d = N_HEADS * HEAD_DIM
    gd = N_GROUPS * HEAD_DIM
    bd, qg = _qk_side_inputs(q_gain)
    kg = jnp.tile(k_gain.astype(F32), N_GROUPS)[None, :]
    tm = MOBA_BLOCK
    nt = S // tm
    return pl.pallas_call(
        _moba_proj_kernel,
        grid=(B * nt,),
        in_specs=[
            pl.BlockSpec((tm, D), lambda i: (i, 0)),
            pl.BlockSpec((1, D), lambda i: (0, 0)),
            pl.BlockSpec((1, D), lambda i: (0, 0)),
            pl.BlockSpec((D, d), lambda i: (0, 0)),
            pl.BlockSpec((D, 2 * gd), lambda i: (0, 0)),
            pl.BlockSpec((LANES, LANES), lambda i: (0, 0)),
            pl.BlockSpec((1, d), lambda i: (0, 0)),
            pl.BlockSpec((1, gd), lambda i: (0, 0)),
        ],
        out_specs=[
            pl.BlockSpec((None, N_HEADS, LANES, tm), lambda i: (i // nt, 0, 0, i % nt)),
            pl.BlockSpec((None, N_GROUPS, tm, LANES), lambda i: (i // nt, 0, i % nt, 0)),
            pl.BlockSpec((None, N_GROUPS, HEAD_DIM, tm), lambda i: (i // nt, 0, 0, i % nt)),
            pl.BlockSpec((None, 1, gd), lambda i: (i, 0, 0)),
        ],
        out_shape=[
            jax.ShapeDtypeStruct((B, N_HEADS, LANES, S), BF16),
            jax.ShapeDtypeStruct((B, N_GROUPS, S, LANES), BF16),
            jax.ShapeDtypeStruct((B, N_GROUPS, HEAD_DIM, S), BF16),
            jax.ShapeDtypeStruct((B * nt, 1, gd), F32),
        ],
        compiler_params=_cparams(1),
    )(h2, g_mix[None, :].astype(F32), g_kv[None, :].astype(F32), w_q.astype(BF16), kv_w.astype(BF16),
      bd, qg, kg)


def _compress_kernel(x_ref, pos_ref, w1_ref, w2_ref, kg_ref, kc_ref, vct_ref, *, n_chunk):
    t = pl.program_id(1)
    half = CMP_BLOCK // 2
    za = [[], []]
    zb = [[], []]
    for l in range(half):
        for p in range(2):
            xl = x_ref[p, pl.ds(l, n_chunk, stride=CMP_STRIDE), :]
            za[p].append((xl + pos_ref[l:l + 1, p * LANES:(p + 1) * LANES]).astype(BF16))
            zb[p].append((xl + pos_ref[half + l:half + l + 1, p * LANES:(p + 1) * LANES]).astype(BF16))
    za = [jnp.concatenate(z, axis=1) for z in za]
    zb = [jnp.concatenate(z, axis=1) for z in zb]
    for g in range(N_GROUPS):
        p, e = g // 2, g % 2
        first = _dot(za[p], w1_ref[0, e])
        second = _dot(zb[p], w1_ref[1, e])
        hid = first + pltpu.roll(second, n_chunk - 1, 0)
        hid = hid * (1.0 / (1.0 + jnp.exp(-hid)))
        out = _dot(hid.astype(BF16), w2_ref[...])

        @pl.when(t == 0)
        def _():
            normed = out * lax.rsqrt(jnp.mean(out * out, axis=-1, keepdims=True) + RMS_EPS) * kg_ref[...]
            kc_ref[g] = normed.astype(kc_ref.dtype)

        @pl.when(t == 1)
        def _():
            vct_ref[g] = out.T[:HEAD_DIM, :].astype(vct_ref.dtype)


def _nsa_compress(cmp_raw, cmp_pos, cmp_w1, cmp_w2, k_gain0, B, S):
    gd = N_GROUPS * HEAD_DIM
    half = CMP_BLOCK // 2
    n_chunk = S // CMP_STRIDE
    pos = jnp.tile(cmp_pos.astype(F32), (1, 1, N_GROUPS))
    w1 = cmp_w1.astype(F32).reshape(2, 2, half, HEAD_DIM, CMP_HIDDEN)
    z = jnp.zeros_like(w1)
    w1p = jnp.stack([jnp.concatenate([w1, z], axis=3), jnp.concatenate([z, w1], axis=3)], axis=2)
    w1p = w1p.reshape(2, 2, 2, half * LANES, CMP_HIDDEN).astype(BF16)
    w2 = jnp.concatenate([cmp_w2, cmp_w2], axis=-1).astype(BF16)
    kg = jnp.tile(k_gain0.astype(F32), 2)[None, :]
    return pl.pallas_call(
        functools.partial(_compress_kernel, n_chunk=n_chunk),
        grid=(B, 2),
        in_specs=[
            pl.BlockSpec((2, S, LANES), lambda b, t: (t, b, 0)),
            pl.BlockSpec((None, CMP_BLOCK, gd), lambda b, t: (t, 0, 0)),
            pl.BlockSpec((None, 2, 2, half * LANES, CMP_HIDDEN), lambda b, t: (t, 0, 0, 0, 0)),
            pl.BlockSpec((None, CMP_HIDDEN, LANES), lambda b, t: (t, 0, 0)),
            pl.BlockSpec((1, LANES), lambda b, t: (0, 0)),
        ],
        out_specs=[
            pl.BlockSpec((None, N_GROUPS, n_chunk, LANES), lambda b, t: (b, 0, 0, 0)),
            pl.BlockSpec((None, N_GROUPS, HEAD_DIM, n_chunk), lambda b, t: (b, 0, 0, 0)),
        ],
        out_shape=[
            jax.ShapeDtypeStruct((B, N_GROUPS, n_chunk, LANES), BF16),
            jax.ShapeDtypeStruct((B, N_GROUPS, HEAD_DIM, n_chunk), BF16),
        ],
        compiler_params=_cparams(2),
    )(cmp_raw, pos, w1p, w2, kg)


def _load_qt(q_ref):
    return jnp.concatenate([q_ref[r] for r in range(R)], axis=1)


def _topk_mask_t(score, k):
    row = lax.broadcasted_iota(jnp.int32, score.shape, 0).astype(F32)
    sel = jnp.zeros(score.shape, F32)
    s = score
    for _ in range(k):
        m = jnp.max(s, axis=0, keepdims=True)
        idx = jnp.min(jnp.where(s == m, row, float(LANES)), axis=0, keepdims=True)
        pick = row == idx
        sel = jnp.where(pick, 1.0, sel)
        s = jnp.where(pick, -jnp.inf, s)
    return sel


def _store_heads_out(o_ref, ot, gate_t, gate_col):
    parts = []
    for r in range(R):
        o_r = ot[:, r * TQ:(r + 1) * TQ]
        if gate_t is not None:
            c = 3 * r + gate_col
            o_r = o_r * gate_t[c:c + 1, :]
        parts.append(o_r)
    o_ref[...] = jnp.concatenate(parts, axis=0)


def _flash_tiles(qa, k_ref, oh_ref, vt_ref, strip_ref, q0, n_tiles):
    def body(kt, carry):
        m, l, acc = carry
        k0 = pl.multiple_of(kt * TK, TK)
        ka = jnp.concatenate([oh_ref[pl.ds(k0, TK), :], k_ref[pl.ds(k0, TK), :]], axis=1)
        cs = pl.multiple_of(DELTA_MAX - jnp.minimum(q0 - k0, DELTA_MAX), LANES)
        s = _dot(ka, qa) + strip_ref[pl.ds(cs, TK), :]
        m_new = jnp.maximum(m, jnp.max(s, axis=0, keepdims=True))
        alpha = jnp.exp2(m - m_new)
        p = jnp.exp2(s - m_new)
        l = alpha * l + jnp.sum(p, axis=0, keepdims=True)
        acc = alpha * acc + _dot(vt_ref[:, pl.ds(k0, TK)], p.astype(BF16))
        return m_new, l, acc

    init = (jnp.full((1, ROWS), 3 * NEG, F32), jnp.zeros((1, ROWS), F32), jnp.zeros((HEAD_DIM, ROWS), F32))
    m, l, acc = lax.fori_loop(0, n_tiles, body, init)
    return acc * (1.0 / l)


def _nsa_cmp_kernel(q_ref, kc_ref, vct_ref, c2st_ref, gate_ref, o_ref, mb_ref, *, n_chunk):
    q0 = pl.program_id(2) * TQ
    qt = _load_qt(q_ref)
    lc = _dot(kc_ref[...], qt)
    n = lax.broadcasted_iota(jnp.int32, (n_chunk, ROWS), 0)
    t = q0 + (lax.broadcasted_iota(jnp.int32, (n_chunk, ROWS), 1) & (TQ - 1))
    mask = n * CMP_STRIDE + (CMP_BLOCK - 1) <= t
    lcm = jnp.where(mask, lc, NEG)
    e = jnp.exp2(lcm - jnp.max(lcm, axis=0, keepdims=True))
    pc = jnp.where(mask, e * (1.0 / jnp.sum(e, axis=0, keepdims=True)), 0.0)
    oc = _dot(vct_ref[...], pc.astype(BF16))
    _store_heads_out(o_ref, oc, gate_ref[...], 0)
    psum = pc[:, 0:TQ]
    for r in range(1, R):
        psum = psum + pc[:, r * TQ:(r + 1) * TQ]
    imp = None
    for part in _split3(psum):
        term = _dot(c2st_ref[...], part)
        imp = term if imp is None else imp + term
    cur = (q0 + lax.broadcasted_iota(jnp.int32, (LANES, TQ), 1)) // SEL_BLOCK
    j = lax.broadcasted_iota(jnp.int32, (LANES, TQ), 0)
    forced_or_imp = jnp.where(j == 0, BIG, jnp.where(j >= cur - 1, BIG, imp))
    score = jnp.where(j <= cur, forced_or_imp, -BIG)
    sel = _topk_mask_t(score, SEL_TOPN)
    mb_ref[...] = jnp.where(sel > 0, 0.0, NEG).astype(mb_ref.dtype)


def _nsa_cmp(qt, kc, vct, gates, B, S):
    n_chunk = S // CMP_STRIDE
    n_sel = S // SEL_BLOCK
    assert n_sel <= LANES and n_chunk % LANES == 0
    cs = jnp.arange(n_chunk)[None, :] * CMP_STRIDE
    ss = jnp.arange(LANES)[:, None] * SEL_BLOCK
    overlap = jnp.clip(jnp.minimum(cs + CMP_BLOCK, ss + SEL_BLOCK) - jnp.maximum(cs, ss), 0, None) / CMP_BLOCK
    n_cmp = (S - CMP_BLOCK) // CMP_STRIDE + 1
    c2st = jnp.where((jnp.arange(n_chunk)[None, :] < n_cmp) & (jnp.arange(LANES)[:, None] < n_sel),
                     overlap, 0.0).astype(BF16)
    nq = S // TQ
    return pl.pallas_call(
        functools.partial(_nsa_cmp_kernel, n_chunk=n_chunk),
        grid=(B, N_GROUPS, nq),
        in_specs=[
            pl.BlockSpec((None, R, LANES, TQ), lambda b, g, i: (b, g, 0, i)),
            pl.BlockSpec((None, None, n_chunk, LANES), lambda b, g, i: (b, g, 0, 0)),
            pl.BlockSpec((None, None, HEAD_DIM, n_chunk), lambda b, g, i: (b, g, 0, 0)),
            pl.BlockSpec((LANES, n_chunk), lambda b, g, i: (0, 0)),
            pl.BlockSpec((None, None, GATE_ROWS, TQ), lambda b, g, i: (b, g, 0, i)),
        ],
        out_specs=[
            pl.BlockSpec((None, R * HEAD_DIM, TQ), lambda b, g, i: (b, g, i)),
            pl.BlockSpec((None, None, LANES, TQ), lambda b, g, i: (b, g, 0, i)),
        ],
        out_shape=[
            jax.ShapeDtypeStruct((B, N_HEADS * HEAD_DIM, S), F32),
            jax.ShapeDtypeStruct((B, N_GROUPS, LANES, S), BF16),
        ],
        compiler_params=_cparams(3),
    )(qt, kc, vct, c2st, gates)


def _nsa_sel_kernel(q_ref, mb_ref, k_ref, oh_ref, vt_ref, strip_ref, gate_ref, o_ref):
    q0 = pl.program_id(2) * TQ
    mb = mb_ref[...]
    qa = jnp.concatenate([jnp.concatenate([mb] * R, axis=1), _load_qt(q_ref)], axis=0)
    n_tiles = (q0 + TQ + TK - 1) // TK
    o = _flash_tiles(qa, k_ref, oh_ref, vt_ref, strip_ref, q0, n_tiles)
    _store_heads_out(o_ref, o, gate_ref[...], 1)


def _nsa_sel(qt, mb, ksel, vsel_t, strip, gates, B, S):
    onehot = (jnp.arange(S)[:, None] // SEL_BLOCK == jnp.arange(LANES)[None, :]).astype(BF16)
    nq = S // TQ
    return pl.pallas_call(
        _nsa_sel_kernel,
        grid=(B, N_GROUPS, nq),
        in_specs=[
            pl.BlockSpec((None, R, LANES, TQ), lambda b, g, i: (b, g, 0, i)),
            pl.BlockSpec((None, None, LANES, TQ), lambda b, g, i: (b, g, 0, i)),
            pl.BlockSpec((None, None, S, LANES), lambda b, g, i: (b, g, 0, 0)),
            pl.BlockSpec((S, LANES), lambda b, g, i: (0, 0)),
            pl.BlockSpec((None, None, HEAD_DIM, S), lambda b, g, i: (b, g, 0, 0)),
            pl.BlockSpec((None, STRIP_W, ROWS), lambda b, g, i: (g, 0, 0)),
            pl.BlockSpec((None, None, GATE_ROWS, TQ), lambda b, g, i: (b, g, 0, i)),
        ],
        out_specs=pl.BlockSpec((None, R * HEAD_DIM, TQ), lambda b, g, i: (b, g, i)),
        out_shape=jax.ShapeDtypeStruct((B, N_HEADS * HEAD_DIM, S), F32),
        compiler_params=_cparams(3),
    )(qt, mb, ksel, onehot, vsel_t, strip, gates)


def _nsa_win_kernel(q_ref, k_ref, vt_ref, strip_ref, gate_ref, o_ref):
    q0 = pl.program_id(2) * TQ
    qt = _load_qt(q_ref)
    k0 = pl.multiple_of(jnp.maximum(q0 - WINDOW, 0), TQ)
    cs = pl.multiple_of(WINDOW - (q0 - k0), LANES)
    s = _dot(k_ref[pl.ds(k0, WIN_KEYS), :], qt) + strip_ref[pl.ds(cs, WIN_KEYS), :]
    e = jnp.exp2(s - jnp.max(s, axis=0, keepdims=True))
    p = e * (1.0 / jnp.sum(e, axis=0, keepdims=True))
    o = _dot(vt_ref[:, pl.ds(k0, WIN_KEYS)], p.astype(BF16))
    _store_heads_out(o_ref, o, gate_ref[...], 2)


def _nsa_win(qt, kwin, vwin_t, strip, gates, B, S):
    assert S >= WIN_KEYS
    nq = S // TQ
    return pl.pallas_call(
        _nsa_win_kernel,
        grid=(B, N_GROUPS, nq),
        in_specs=[
            pl.BlockSpec((None, R, LANES, TQ), lambda b, g, i: (b, g, 0, i)),
            pl.BlockSpec((None, None, S, LANES), lambda b, g, i: (b, g, 0, 0)),
            pl.BlockSpec((None, None, HEAD_DIM, S), lambda b, g, i: (b, g, 0, 0)),
            pl.BlockSpec((None, WIN_STRIP_W, ROWS), lambda b, g, i: (g, 0, 0)),
            pl.BlockSpec((None, None, GATE_ROWS, TQ), lambda b, g, i: (b, g, 0, i)),
        ],
        out_specs=pl.BlockSpec((None, R * HEAD_DIM, TQ), lambda b, g, i: (b, g, i)),
        out_shape=jax.ShapeDtypeStruct((B, N_HEADS * HEAD_DIM, S), F32),
        compiler_params=_cparams(3),
    )(qt, kwin, vwin_t, strip, gates)


def _moba_kernel(q_ref, kmean_ref, k_ref, oh_ref, vt_ref, strip_ref, o_ref):
    q0 = pl.program_id(2) * TQ
    cblk = q0 // MOBA_BLOCK
    qt = _load_qt(q_ref)
    gate = _dot(kmean_ref[...], qt)
    n = lax.broadcasted_iota(jnp.int32, (LANES, ROWS), 0)
    past = n < cblk
    sel = _topk_mask_t(jnp.where(past, gate, NEG), MOBA_TOPK)
    mb = jnp.where(n == cblk, 0.0, jnp.where(past, jnp.where(sel > 0, 0.0, NEG), NEG)).astype(BF16)
    qa = jnp.concatenate([mb, qt], axis=0)
    n_tiles = (q0 + TQ + TK - 1) // TK
    o = _flash_tiles(qa, k_ref, oh_ref, vt_ref, strip_ref, q0, n_tiles)
    _store_heads_out(o_ref, o, None, 0)


def _moba_attn(qt, kmean, k, vt, strip, B, S):
    n_blk = S // MOBA_BLOCK
    assert n_blk <= LANES
    onehot = (jnp.arange(S)[:, None] // MOBA_BLOCK == jnp.arange(LANES)[None, :]).astype(BF16)
    nq = S // TQ
    return pl.pallas_call(
        _moba_kernel,
        grid=(B, N_GROUPS, nq),
        in_specs=[
            pl.BlockSpec((None, R, LANES, TQ), lambda b, g, i: (b, g, 0, i)),
            pl.BlockSpec((None, None, LANES, LANES), lambda b, g, i: (b, g, 0, 0)),
            pl.BlockSpec((None, None, S, LANES), lambda b, g, i: (b, g, 0, 0)),
            pl.BlockSpec((S, LANES), lambda b, g, i: (0, 0)),
            pl.BlockSpec((None, None, HEAD_DIM, S), lambda b, g, i: (b, g, 0, 0)),
            pl.BlockSpec((None, STRIP_W, ROWS), lambda b, g, i: (g, 0, 0)),
        ],
        out_specs=pl.BlockSpec((None, R * HEAD_DIM, TQ), lambda b, g, i: (b, g, i)),
        out_shape=jax.ShapeDtypeStruct((B, N_HEADS * HEAD_DIM, S), F32),
        compiler_params=_cparams(3),
    )(qt, kmean, k, onehot, vt, strip)


def _out_proj_kernel(*refs):
    *o_refs, w_ref, h_ref, out_ref = refs
    o = o_refs[0][...]
    for r in o_refs[1:]:
        o = o + r[...]
    out_ref[...] = h_ref[...] + _dot(o.T.astype(BF16), w_ref[...])


def _out_proj(parts_t, w_out, h2, B, S):
    d = parts_t[0].shape[1]
    D = w_out.shape[1]
    tm = 512
    nt = S // tm
    return pl.pallas_call(
        _out_proj_kernel,
        grid=(B * nt,),
        in_specs=[pl.BlockSpec((None, d, tm), lambda i: (i // nt, 0, i % nt)) for _ in parts_t] + [
            pl.BlockSpec((d, D), lambda i: (0, 0)),
            pl.BlockSpec((tm, D), lambda i: (i, 0)),
        ],
        out_specs=pl.BlockSpec((tm, D), lambda i: (i, 0)),
        out_shape=jax.ShapeDtypeStruct((B * S, D), F32),
        compiler_params=_cparams(1),
    )(*parts_t, w_out.astype(BF16), h2)


def _mlp_kernel(h_ref, g_ref, wu_ref, wd_ref, out_ref, *, ff_chunk):
    h = h_ref[...]
    ms = jnp.mean(h * h, axis=-1, keepdims=True)
    xn = (h * lax.rsqrt(ms + RMS_EPS) * g_ref[...]).astype(BF16)
    acc = h
    for c in range(wu_ref.shape[1] // ff_chunk):
        a = jnp.maximum(_dot(xn, wu_ref[:, c * ff_chunk:(c + 1) * ff_chunk]), 0.0)
        acc = acc + _dot((a * a).astype(BF16), wd_ref[c * ff_chunk:(c + 1) * ff_chunk, :])
    out_ref[...] = acc


def _mlp(h2, gain, w_up, w_down):
    n, D = h2.shape
    F = w_up.shape[1]
    tm = 256
    return pl.pallas_call(
        functools.partial(_mlp_kernel, ff_chunk=1024),
        grid=(n // tm,),
        in_specs=[
            pl.BlockSpec((tm, D), lambda i: (i, 0)),
            pl.BlockSpec((1, D), lambda i: (0, 0)),
            pl.BlockSpec((D, F), lambda i: (0, 0)),
            pl.BlockSpec((F, D), lambda i: (0, 0)),
        ],
        out_specs=pl.BlockSpec((tm, D), lambda i: (i, 0)),
        out_shape=jax.ShapeDtypeStruct((n, D), F32),
        compiler_params=_cparams(1),
    )(h2, gain[None, :].astype(F32), w_up.astype(BF16), w_down.astype(BF16))


def _nsa_layer(h2, B, S, norm_mix, w_in, q_gain, k_gain, cmp_pos, cmp_w1, cmp_w2, w_out, sel_strip, win_strip):
    qt, cmp_raw, ksel, vsel_t, kwin, vwin_t, gates = _nsa_project(h2, norm_mix, w_in, q_gain, k_gain, B, S)
    kc, vct = _nsa_compress(cmp_raw, cmp_pos, cmp_w1, cmp_w2, k_gain[0], B, S)
    o_cmp, mb = _nsa_cmp(qt, kc, vct, gates, B, S)
    o_sel = _nsa_sel(qt, mb, ksel, vsel_t, sel_strip, gates, B, S)
    o_win = _nsa_win(qt, kwin, vwin_t, win_strip, gates, B, S)
    return _out_proj([o_cmp, o_sel, o_win], w_out, h2, B, S)


def _shared_kv_and_q(h2, B, S, norm_mix, kv_norm, kv_w, kv_k_gain, w_q, q_gain):
    qt, k, vt, kmean = _moba_project(h2, norm_mix, kv_norm, w_q, kv_w, q_gain, kv_k_gain, B, S)
    n_blk = S // MOBA_BLOCK
    km = kmean.reshape(B, n_blk, N_GROUPS, HEAD_DIM).transpose(0, 2, 1, 3)
    km = jnp.concatenate([km, km], axis=-1)
    km = jnp.pad(km, ((0, 0), (0, 0), (0, LANES - n_blk), (0, 0))).astype(BF16)
    return qt, k, vt, km


def kernel(x, norm_mix, norm_mlp, nsa_w_in, nsa_q_gain, nsa_k_gain, nsa_cmp_pos, nsa_cmp_w1, nsa_cmp_w2,
           nsa_w_out, kv_norm, kv_w, kv_k_gain, moba_w_q, moba_q_gain, moba_w_out, rel_table, mlp_w_up,
           mlp_w_down):
    B, S, D = x.shape
    depth = norm_mix.shape[0]
    n_a = nsa_w_in.shape[0]
    assert S % TK == 0 and S % MOBA_BLOCK == 0
    sel_strip = _bias_strip(rel_table, STRIP_W, DELTA_MAX, 1 << 30)
    win_strip = _bias_strip(rel_table, WIN_STRIP_W, WINDOW, WINDOW)
    h2 = x.reshape(B * S, D)
    shared = None
    for layer in range(depth):
        if layer < n_a:
            i = layer
            h2 = _nsa_layer(h2, B, S, norm_mix[layer], nsa_w_in[i], nsa_q_gain[i], nsa_k_gain[i],
                            nsa_cmp_pos[i], nsa_cmp_w1[i], nsa_cmp_w2[i], nsa_w_out[i], sel_strip, win_strip)
        else:
            j = layer - n_a
            if shared is None:
                qt, k, vt, km = _shared_kv_and_q(h2, B, S, norm_mix[layer], kv_norm, kv_w, kv_k_gain,
                                                 moba_w_q[j], moba_q_gain[j])
                shared = (k, vt, km)
            else:
                qt = _moba_project(h2, norm_mix[layer], kv_norm, moba_w_q[j], kv_w, moba_q_gain[j],
                                   kv_k_gain, B, S)[0]
            k, vt, km = shared
            o = _moba_attn(qt, km, k, vt, sel_strip, B, S)
            h2 = _out_proj([o], moba_w_out[j], h2, B, S)
        h2 = _mlp(h2, norm_mlp[layer], mlp_w_up[layer], mlp_w_down[layer])
    return h2.reshape(B, S, D)
```

```python
import functools
import math

import jax
import jax.numpy as jnp
from jax import lax
from jax.experimental import pallas as pl
from jax.experimental.pallas import tpu as pltpu

F32 = jnp.float32
BF16 = jnp.bfloat16

N_HEADS = 16
HEAD_DIM = 64
N_GROUPS = 4
R = N_HEADS // N_GROUPS
LANES = 128
CMP_BLOCK = 32
CMP_STRIDE = 16
CMP_HIDDEN = 4 * HEAD_DIM
SEL_BLOCK = 64
SEL_TOPN = 16
N_FORCED = 3
WINDOW = 512
MOBA_BLOCK = 256
MOBA_TOPK = 3
REL_BUCKETS = 32
REL_MAX_DIST = 4096
RMS_EPS = 1e-6
NEG = -1e30
BIG = 1e9
LOG2E = math.log2(math.e)
GATE_ROWS = 16
V_ROWS = HEAD_DIM + 16

TQ = 128
QSTEP = 512
NSUB = QSTEP // TQ
TK = 512
ROWS = R * TQ
QK_AHEAD = 4
FAR_DIST = 2897
DELTA_MAX = -(-(FAR_DIST + TK - 1) // LANES) * LANES
STRIP_W = -(-(DELTA_MAX + TK) // 256) * 256
WIN_KEYS = WINDOW + TQ
WIN_STRIP_W = -(-(WINDOW + WIN_KEYS) // 256) * 256
VMEM_LIMIT = 56 * 1024 * 1024


def _cparams(n_axes, flags=None):
    return pltpu.CompilerParams(dimension_semantics=("arbitrary",) * n_axes,
                                vmem_limit_bytes=VMEM_LIMIT, flags=flags)


_INTERLEAVE = None


def _dot(a, b):
    return jnp.dot(a, b, preferred_element_type=F32)


def _split3(x):
    parts = []
    rem = x
    for _ in range(3):
        hi = rem.astype(BF16)
        parts.append(hi)
        rem = rem - hi.astype(F32)
    return parts


def _strip_kernel(tab_ref, o_ref, *, c0, dlimit, ch):
    g = pl.program_id(0)
    j = pl.program_id(1)
    r = pl.program_id(2)
    h = g * R + r
    shape = (ch, TQ)
    c = lax.broadcasted_iota(jnp.int32, shape, 0) + j * ch
    i = lax.broadcasted_iota(jnp.int32, shape, 1)
    d = i - c + c0
    n = jnp.maximum(d, 0)
    max_exact = REL_BUCKETS // 2
    nf = jnp.maximum(n, max_exact).astype(F32)
    large = max_exact + (jnp.log(nf / max_exact) / math.log(REL_MAX_DIST / max_exact)
                         * (REL_BUCKETS - max_exact)).astype(jnp.int32)
    large = jnp.minimum(large, REL_BUCKETS - 1)
    bucket = jnp.where(n < max_exact, n, large)
    val = jnp.zeros(shape, F32)
    for b in range(REL_BUCKETS):
        val = jnp.where(bucket == b, tab_ref[b, h], val)
    val = val - tab_ref[REL_BUCKETS - 1, h]
    ok = jnp.where(d >= 0, jnp.where(d < dlimit, 1, 0), 0)
    o_ref[...] = jnp.where(ok > 0, val * LOG2E, NEG)


def _bias_strip(rel_table, width, c0, dlimit):
    ch = 256
    return pl.pallas_call(
        functools.partial(_strip_kernel, c0=c0, dlimit=dlimit, ch=ch),
        grid=(N_GROUPS, width // ch, R),
        in_specs=[pl.BlockSpec(memory_space=pltpu.SMEM)],
        out_specs=pl.BlockSpec((None, ch, TQ), lambda g, j, r: (g, j, r)),
        out_shape=jax.ShapeDtypeStruct((N_GROUPS, width, ROWS), F32),
        compiler_params=_cparams(3),
    )(rel_table.astype(F32))


def _head_norm(y, bd_ref, gain):
    outs = []
    for c in range(y.shape[1] // LANES):
        yc = y[:, c * LANES:(c + 1) * LANES]
        y2 = yc * yc
        hi = y2.astype(BF16)
        lo = (y2 - hi.astype(F32)).astype(BF16)
        ss = _dot(hi, bd_ref[...]) + _dot(lo, bd_ref[...])
        outs.append(yc * lax.rsqrt(ss * (1.0 / HEAD_DIM) + RMS_EPS))
    return jnp.concatenate(outs, axis=1) * gain


def _store_heads_t(q_ref, y):
    yt = y.T
    row = lax.broadcasted_iota(jnp.int32, (LANES, y.shape[0]), 0)
    for h in range(N_HEADS):
        pair = yt[(h // 2) * LANES:(h // 2 + 1) * LANES, :]
        keep = row < HEAD_DIM if h % 2 == 0 else row >= HEAD_DIM
        q_ref[h] = jnp.where(keep, pair, 0.0).astype(q_ref.dtype)


def _store_groups_dup(o_ref, y):
    lane = lax.broadcasted_iota(jnp.int32, (y.shape[0], LANES), 1)
    for g in range(N_GROUPS):
        pair = y[:, (g // 2) * LANES:(g // 2 + 1) * LANES]
        keep = lane < HEAD_DIM if g % 2 == 0 else lane >= HEAD_DIM
        m = jnp.where(keep, pair, 0.0)
        o_ref[g] = (m + pltpu.roll(m, HEAD_DIM, 1)).astype(o_ref.dtype)


def _store_groups_t(o_ref, y):
    yt = y.T
    ones = jnp.ones((V_ROWS - HEAD_DIM, y.shape[0]), o_ref.dtype)
    for g in range(N_GROUPS):
        o_ref[g, 0:HEAD_DIM, :] = yt[g * HEAD_DIM:(g + 1) * HEAD_DIM, :].astype(o_ref.dtype)
        o_ref[g, HEAD_DIM:V_ROWS, :] = ones


def _nsa_proj_kernel(x_ref, gn_ref, w_ref, bd_ref, qg_ref, kg_ref,
                     q_ref, cmp_ref, ksel_ref, vsel_ref, kwin_ref, vwin_ref, gate_ref):
    x = x_ref[...]
    ms = jnp.mean(x * x, axis=-1, keepdims=True)
    xn = (x * lax.rsqrt(ms + RMS_EPS) * gn_ref[...]).astype(BF16)
    d = N_HEADS * HEAD_DIM
    gd = N_GROUPS * HEAD_DIM
    yq = _dot(xn, w_ref[:, 0:d])
    _store_heads_t(q_ref, _head_norm(yq, bd_ref, qg_ref[...]))
    ycmp = _dot(xn, w_ref[:, d:d + 2 * gd])
    for c in range(2 * gd // LANES):
        cmp_ref[c] = ycmp[:, c * LANES:(c + 1) * LANES]
    ysel = _dot(xn, w_ref[:, d + 2 * gd:d + 4 * gd])
    _store_groups_dup(ksel_ref, _head_norm(ysel[:, :gd], bd_ref, kg_ref[0:1, :]))
    _store_groups_t(vsel_ref, ysel[:, gd:])
    ywin = _dot(xn, w_ref[:, d + 4 * gd:d + 6 * gd])
    _store_groups_dup(kwin_ref, _head_norm(ywin[:, :gd], bd_ref, kg_ref[1:2, :]))
    _store_groups_t(vwin_ref, ywin[:, gd:])
    yg = _dot(xn, w_ref[:, d + 6 * gd:])
    sgt = (1.0 / (1.0 + jnp.exp(-yg))).T
    for g in range(N_GROUPS):
        gate_ref[g] = sgt[g * LANES:g * LANES + GATE_ROWS, :]


def _qk_side_inputs(q_gain):
    bd = jnp.kron(jnp.eye(2, dtype=F32), jnp.ones((HEAD_DIM, HEAD_DIM), F32)).astype(BF16)
    qg = jnp.tile(q_gain.astype(F32) * (HEAD_DIM ** -0.5 * LOG2E), N_HEADS)[None, :]
    return bd, qg


def _nsa_project(x2, gain, w_in, q_gain, k_gain, B, S):
    D = x2.shape[1]
    d = N_HEADS * HEAD_DIM
    gd = N_GROUPS * HEAD_DIM
    wg = w_in[:, d + 6 * gd:].reshape(D, N_GROUPS, 3 * R)
    wg = jnp.pad(wg, ((0, 0), (0, 0), (0, LANES - 3 * R))).reshape(D, N_GROUPS * LANES)
    w = jnp.concatenate([w_in[:, :d + 6 * gd], wg], axis=1).astype(BF16)
    ncol = w.shape[1]
    bd, qg = _qk_side_inputs(q_gain)
    kg = jnp.stack([jnp.tile(k_gain[1].astype(F32), N_GROUPS), jnp.tile(k_gain[2].astype(F32), N_GROUPS)])
    tm = 256
    nt = S // tm
    kspec = pl.BlockSpec((None, N_GROUPS, tm, LANES), lambda i: (i // nt, 0, i % nt, 0))
    vspec = pl.BlockSpec((None, N_GROUPS, V_ROWS, tm), lambda i: (i // nt, 0, 0, i % nt))
    kshape = jax.ShapeDtypeStruct((B, N_GROUPS, S, LANES), BF16)
    vshape = jax.ShapeDtypeStruct((B, N_GROUPS, V_ROWS, S), BF16)
    return pl.pallas_call(
        _nsa_proj_kernel,
        grid=(B * nt,),
        in_specs=[
            pl.BlockSpec((tm, D), lambda i: (i, 0)),
            pl.BlockSpec((1, D), lambda i: (0, 0)),
            pl.BlockSpec((D, ncol), lambda i: (0, 0)),
            pl.BlockSpec((LANES, LANES), lambda i: (0, 0)),
            pl.BlockSpec((1, d), lambda i: (0, 0)),
            pl.BlockSpec((2, gd), lambda i: (0, 0)),
        ],
        out_specs=[
            pl.BlockSpec((None, N_HEADS, LANES, tm), lambda i: (i // nt, 0, 0, i % nt)),
            pl.BlockSpec((2 * gd // LANES, tm, LANES), lambda i: (0, i, 0)),
            kspec, vspec, kspec, vspec,
            pl.BlockSpec((None, N_GROUPS, GATE_ROWS, tm), lambda i: (i // nt, 0, 0, i % nt)),
        ],
        out_shape=[
            jax.ShapeDtypeStruct((B, N_HEADS, LANES, S), BF16),
            jax.ShapeDtypeStruct((2 * gd // LANES, B * S, LANES), F32),
            kshape, vshape, kshape, vshape,
            jax.ShapeDtypeStruct((B, N_GROUPS, GATE_ROWS, S), F32),
        ],
        compiler_params=_cparams(1),
    )(x2, gain[None, :].astype(F32), w, bd, qg, kg)


def _moba_proj_kernel(x_ref, gq_ref, gkv_ref, wq_ref, wkv_ref, bd_ref, qg_ref, kg_ref,
                      q_ref, k_ref, v_ref, kmean_ref):
    x = x_ref[...]
    ms = jnp.mean(x * x, axis=-1, keepdims=True)
    xr = x * lax.rsqrt(ms + RMS_EPS)
    gd = N_GROUPS * HEAD_DIM
    yq = _dot((xr * gq_ref[...]).astype(BF16), wq_ref[...])
    _store_heads_t(q_ref, _head_norm(yq, bd_ref, qg_ref[...]))
    ykv = _dot((xr * gkv_ref[...]).astype(BF16), wkv_ref[...])
    kn = _head_norm(ykv[:, :gd], bd_ref, kg_ref[...])
    _store_groups_dup(k_ref, kn)
    _store_groups_t(v_ref, ykv[:, gd:])
    kmean_ref[...] = jnp.mean(kn, axis=0, keepdims=True)


def _moba_project(h2, g_mix, g_kv, w_q, kv_w, q_gain, k_gain, B, S):
    D = h2.shape[1]
    d = N_HEADS * HEAD_DIM
    gd = N_GROUPS * HEAD_DIM
    bd, qg = _qk_side_inputs(q_gain)
    kg = jnp.tile(k_gain.astype(F32), N_GROUPS)[None, :]
    tm = MOBA_BLOCK
    nt = S // tm
    return pl.pallas_call(
        _moba_proj_kernel,
        grid=(B * nt,),
        in_specs=[
            pl.BlockSpec((tm, D), lambda i: (i, 0)),
            pl.BlockSpec((1, D), lambda i: (0, 0)),
            pl.BlockSpec((1, D), lambda i: (0, 0)),
            pl.BlockSpec((D, d), lambda i: (0, 0)),
            pl.BlockSpec((D, 2 * gd), lambda i: (0, 0)),
            pl.BlockSpec((LANES, LANES), lambda i: (0, 0)),
            pl.BlockSpec((1, d), lambda i: (0, 0)),
            pl.BlockSpec((1, gd), lambda i: (0, 0)),
        ],
        out_specs=[
            pl.BlockSpec((None, N_HEADS, LANES, tm), lambda i: (i // nt, 0, 0, i % nt)),
            pl.BlockSpec((None, N_GROUPS, tm, LANES), lambda i: (i // nt, 0, i % nt, 0)),
            pl.BlockSpec((None, N_GROUPS, V_ROWS, tm), lambda i: (i // nt, 0, 0, i % nt)),
            pl.BlockSpec((None, 1, gd), lambda i: (i, 0, 0)),
        ],
        out_shape=[
            jax.ShapeDtypeStruct((B, N_HEADS, LANES, S), BF16),
            jax.ShapeDtypeStruct((B, N_GROUPS, S, LANES), BF16),
            jax.ShapeDtypeStruct((B, N_GROUPS, V_ROWS, S), BF16),
            jax.ShapeDtypeStruct((B * nt, 1, gd), F32),
        ],
        compiler_params=_cparams(1),
    )(h2, g_mix[None, :].astype(F32), g_kv[None, :].astype(F32), w_q.astype(BF16), kv_w.astype(BF16),
      bd, qg, kg)


def _compress_kernel(x_ref, pos_ref, w1_ref, w2_ref, kg_ref, kc_ref, vct_ref, *, n_chunk):
    t = pl.program_id(1)
    half = CMP_BLOCK // 2
    za = [[], []]
    zb = [[], []]
    for l in range(half):
        for p in range(2):
            xl = x_ref[p, pl.ds(l, n_chunk, stride=CMP_STRIDE), :]
            za[p].append((xl + pos_ref[l:l + 1, p * LANES:(p + 1) * LANES]).astype(BF16))
            zb[p].append((xl + pos_ref[half + l:half + l + 1, p * LANES:(p + 1) * LANES]).astype(BF16))
    za = [jnp.concatenate(z, axis=1) for z in za]
    zb = [jnp.concatenate(z, axis=1) for z in zb]
    for g in range(N_GROUPS):
        p, e = g // 2, g % 2
        first = _dot(za[p], w1_ref[0, e])
        second = _dot(zb[p], w1_ref[1, e])
        hid = first + pltpu.roll(second, n_chunk - 1, 0)
        hid = hid * (1.0 / (1.0 + jnp.exp(-hid)))
        out = _dot(hid.astype(BF16), w2_ref[...])

        @pl.when(t == 0)
        def _():
            normed = out * lax.rsqrt(jnp.mean(out * out, axis=-1, keepdims=True) + RMS_EPS) * kg_ref[...]
            kc_ref[g] = normed.astype(kc_ref.dtype)

        @pl.when(t == 1)
        def _():
            vct_ref[g] = out.T[:HEAD_DIM, :].astype(vct_ref.dtype)


def _nsa_compress(cmp_raw, cmp_pos, cmp_w1, cmp_w2, k_gain0, B, S):
    gd = N_GROUPS * HEAD_DIM
    half = CMP_BLOCK // 2
    n_chunk = S // CMP_STRIDE
    pos = jnp.tile(cmp_pos.astype(F32), (1, 1, N_GROUPS))
    w1 = cmp_w1.astype(F32).reshape(2, 2, half, HEAD_DIM, CMP_HIDDEN)
    z = jnp.zeros_like(w1)
    w1p = jnp.stack([jnp.concatenate([w1, z], axis=3), jnp.concatenate([z, w1], axis=3)], axis=2)
    w1p = w1p.reshape(2, 2, 2, half * LANES, CMP_HIDDEN).astype(BF16)
    w2 = jnp.concatenate([cmp_w2, cmp_w2], axis=-1).astype(BF16)
    kg = jnp.tile(k_gain0.astype(F32), 2)[None, :]
    return pl.pallas_call(
        functools.partial(_compress_kernel, n_chunk=n_chunk),
        grid=(B, 2),
        in_specs=[
            pl.BlockSpec((2, S, LANES), lambda b, t: (t, b, 0)),
            pl.BlockSpec((None, CMP_BLOCK, gd), lambda b, t: (t, 0, 0)),
            pl.BlockSpec((None, 2, 2, half * LANES, CMP_HIDDEN), lambda b, t: (t, 0, 0, 0, 0)),
            pl.BlockSpec((None, CMP_HIDDEN, LANES), lambda b, t: (t, 0, 0)),
            pl.BlockSpec((1, LANES), lambda b, t: (0, 0)),
        ],
        out_specs=[
            pl.BlockSpec((None, N_GROUPS, n_chunk, LANES), lambda b, t: (b, 0, 0, 0)),
            pl.BlockSpec((None, N_GROUPS, HEAD_DIM, n_chunk), lambda b, t: (b, 0, 0, 0)),
        ],
        out_shape=[
            jax.ShapeDtypeStruct((B, N_GROUPS, n_chunk, LANES), BF16),
            jax.ShapeDtypeStruct((B, N_GROUPS, HEAD_DIM, n_chunk), BF16),
        ],
        compiler_params=_cparams(2),
    )(cmp_raw, pos, w1p, w2, kg)


_Q_SPEC = pl.BlockSpec((None, R, LANES, QSTEP), lambda b, g, i: (b, g, 0, i))
_GATE_SPEC = pl.BlockSpec((None, None, GATE_ROWS, QSTEP), lambda b, g, i: (b, g, 0, i))
_O_SPEC = pl.BlockSpec((None, R * HEAD_DIM, QSTEP), lambda b, g, i: (b, g, i))


def _chains():
    return [(u, hf) for u in range(NSUB) for hf in range(R // 2)]


def _run_chains(scores, finish):
    chains = _chains()
    ahead = [scores(ch) for ch in chains[:QK_AHEAD]]
    for idx, ch in enumerate(chains):
        s = ahead.pop(0)
        if idx + QK_AHEAD < len(chains):
            ahead.append(scores(chains[idx + QK_AHEAD]))
        finish(ch, s)


def _chain_qt(q_ref, u, hf):
    return jnp.concatenate([q_ref[2 * hf + e, :, u * TQ:(u + 1) * TQ] for e in range(2)], axis=1)


def _store_chain_out(o_ref, ot, gate_ref, gate_col, u, hf):
    for e in range(2):
        r = 2 * hf + e
        o_r = ot[:, e * TQ:(e + 1) * TQ]
        if gate_ref is not None:
            c = 3 * r + gate_col
            o_r = o_r * gate_ref[c:c + 1, u * TQ:(u + 1) * TQ]
        o_ref[r * HEAD_DIM:(r + 1) * HEAD_DIM, u * TQ:(u + 1) * TQ] = o_r


def _topk_mark_t(score, k):
    row = lax.broadcasted_iota(jnp.int32, score.shape, 0).astype(F32)
    s = score
    for _ in range(k):
        m = jnp.max(s, axis=0, keepdims=True)
        idx = jnp.min(jnp.where(s == m, row, float(LANES)), axis=0, keepdims=True)
        s = jnp.where(row == idx, -jnp.inf, s)
    return s


def _flash_tiles(qas, k_ref, oh_ref, vt_ref, strip_ref, q0, n_tiles):
    chains = _chains()
    cw = 2 * TQ

    def body(kt, carry, near):
        k0 = pl.multiple_of(kt * TK, TK)
        ka = jnp.concatenate([oh_ref[pl.ds(k0, TK), :], k_ref[pl.ds(k0, TK), :]], axis=1)
        vt = vt_ref[:, pl.ds(k0, TK)]

        def scores(ch):
            u, hf = ch
            s = _dot(ka, qas[ch])
            if near:
                cs = pl.multiple_of(DELTA_MAX - (q0 + u * TQ - k0), LANES)
                s = s + strip_ref[pl.ds(cs, TK), hf * cw:(hf + 1) * cw]
            return s

        out = []

        def finish(ch, s):
            m, acc = carry[len(out)]
            m_new = jnp.maximum(m, jnp.max(s, axis=0, keepdims=True))
            p = jnp.exp2(s - m_new).astype(BF16)
            out.append((m_new, jnp.exp2(m - m_new) * acc + _dot(vt, p)))

        _run_chains(scores, finish)
        return tuple(out)

    n_far = jnp.maximum(q0 - DELTA_MAX + TK, 0) // TK
    init = tuple((jnp.full((1, cw), 3 * NEG, F32), jnp.zeros((V_ROWS, cw), F32)) for _ in chains)
    mid = lax.fori_loop(0, n_far, functools.partial(body, near=False), init)
    fin = lax.fori_loop(n_far, n_tiles, functools.partial(body, near=True), mid)
    return {ch: acc[:HEAD_DIM] * (1.0 / acc[HEAD_DIM:HEAD_DIM + 1]) for ch, (m, acc) in zip(chains, fin)}


def _nsa_cmp_kernel(q_ref, kc_ref, vct_ref, vis_ref, c2st_ref, gate_ref, o_ref, mb_ref, *, n_chunk):
    q0 = pl.program_id(2) * QSTEP
    cw = 2 * TQ
    i_col = lax.broadcasted_iota(jnp.int32, (1, cw), 1) & (TQ - 1)
    psums = [None] * NSUB

    def scores(ch):
        r0 = pl.multiple_of(n_chunk - (q0 + ch[0] * TQ) // CMP_STRIDE, 8)
        return _dot(kc_ref[...], _chain_qt(q_ref, *ch)) + vis_ref[pl.ds(r0, n_chunk), :]

    def finish(ch, lcm):
        u, hf = ch
        qs = q0 + u * TQ
        e = jnp.exp2(lcm - jnp.max(lcm, axis=0, keepdims=True))
        any_visible = qs + i_col >= CMP_BLOCK - 1
        pc = e * jnp.where(any_visible, 1.0 / jnp.sum(e, axis=0, keepdims=True), 0.0)
        oc = _dot(vct_ref[...], pc.astype(BF16))
        _store_chain_out(o_ref, oc, gate_ref, 0, u, hf)
        ps = pc[:, :TQ] + pc[:, TQ:]
        psums[u] = ps if psums[u] is None else psums[u] + ps

    _run_chains(scores, finish)
    psum = jnp.concatenate(psums, axis=1)
    imp = None
    for part in _split3(psum):
        term = _dot(c2st_ref[...], part)
        imp = term if imp is None else imp + term
    cur = (q0 + lax.broadcasted_iota(jnp.int32, (LANES, QSTEP), 1)) // SEL_BLOCK
    j = lax.broadcasted_iota(jnp.int32, (LANES, QSTEP), 0)
    forced_or_imp = jnp.where(j == 0, -jnp.inf, jnp.where(j >= cur - 1, -jnp.inf, imp))
    score = jnp.where(j <= cur, forced_or_imp, -BIG)
    chosen = _topk_mark_t(score, SEL_TOPN - N_FORCED)
    mb_ref[...] = jnp.where(chosen == -jnp.inf, 0.0, NEG).astype(mb_ref.dtype)


def _nsa_cmp(qt, kc, vct, gates, B, S):
    n_chunk = S // CMP_STRIDE
    n_sel = S // SEL_BLOCK
    assert n_sel <= LANES and n_chunk % LANES == 0
    cs = jnp.arange(n_chunk)[None, :] * CMP_STRIDE
    ss = jnp.arange(LANES)[:, None] * SEL_BLOCK
    overlap = jnp.clip(jnp.minimum(cs + CMP_BLOCK, ss + SEL_BLOCK) - jnp.maximum(cs, ss), 0, None) / CMP_BLOCK
    n_cmp = (S - CMP_BLOCK) // CMP_STRIDE + 1
    c2st = jnp.where((jnp.arange(n_chunk)[None, :] < n_cmp) & (jnp.arange(LANES)[:, None] < n_sel),
                     overlap, 0.0).astype(BF16)
    c_end = (jnp.arange(2 * n_chunk)[:, None] - n_chunk) * CMP_STRIDE + (CMP_BLOCK - 1)
    vis = jnp.where(c_end <= (jnp.arange(2 * TQ)[None, :] & (TQ - 1)), 0.0, NEG).astype(F32)
    return pl.pallas_call(
        functools.partial(_nsa_cmp_kernel, n_chunk=n_chunk),
        grid=(B, N_GROUPS, S // QSTEP),
        in_specs=[
            _Q_SPEC,
            pl.BlockSpec((None, None, n_chunk, LANES), lambda b, g, i: (b, g, 0, 0)),
            pl.BlockSpec((None, None, HEAD_DIM, n_chunk), lambda b, g, i: (b, g, 0, 0)),
            pl.BlockSpec((2 * n_chunk, 2 * TQ), lambda b, g, i: (0, 0)),
            pl.BlockSpec((LANES, n_chunk), lambda b, g, i: (0, 0)),
            _GATE_SPEC,
        ],
        out_specs=[
            _O_SPEC,
            pl.BlockSpec((None, None, LANES, QSTEP), lambda b, g, i: (b, g, 0, i)),
        ],
        out_shape=[
            jax.ShapeDtypeStruct((B, N_HEADS * HEAD_DIM, S), F32),
            jax.ShapeDtypeStruct((B, N_GROUPS, LANES, S), BF16),
        ],
        compiler_params=_cparams(3),
    )(qt, kc, vct, vis, c2st, gates)


def _nsa_sel_kernel(q_ref, mb_ref, k_ref, oh_ref, vt_ref, strip_ref, gate_ref, o_ref):
    q0 = pl.program_id(2) * QSTEP
    qas = {}
    for u, hf in _chains():
        mb = mb_ref[:, u * TQ:(u + 1) * TQ]
        qas[(u, hf)] = jnp.concatenate([jnp.concatenate([mb, mb], axis=1), _chain_qt(q_ref, u, hf)], axis=0)
    n_tiles = q0 // TK + 1
    o = _flash_tiles(qas, k_ref, oh_ref, vt_ref, strip_ref, q0, n_tiles)
    for u, hf in _chains():
        _store_chain_out(o_ref, o[(u, hf)], gate_ref, 1, u, hf)


def _nsa_sel(qt, mb, ksel, vsel_t, strip, gates, B, S):
    onehot = (jnp.arange(S)[:, None] // SEL_BLOCK == jnp.arange(LANES)[None, :]).astype(BF16)
    return pl.pallas_call(
        _nsa_sel_kernel,
        grid=(B, N_GROUPS, S // QSTEP),
        in_specs=[
            _Q_SPEC,
            pl.BlockSpec((None, None, LANES, QSTEP), lambda b, g, i: (b, g, 0, i)),
            pl.BlockSpec((None, None, S, LANES), lambda b, g, i: (b, g, 0, 0)),
            pl.BlockSpec((S, LANES), lambda b, g, i: (0, 0)),
            pl.BlockSpec((None, None, V_ROWS, S), lambda b, g, i: (b, g, 0, 0)),
            pl.BlockSpec((None, STRIP_W, ROWS), lambda b, g, i: (g, 0, 0)),
            _GATE_SPEC,
        ],
        out_specs=_O_SPEC,
        out_shape=jax.ShapeDtypeStruct((B, N_HEADS * HEAD_DIM, S), F32),
        compiler_params=_cparams(3, _INTERLEAVE),
    )(qt, mb, ksel, onehot, vsel_t, strip, gates)


def _nsa_win_kernel(q_ref, k_ref, vt_ref, strip_ref, gate_ref, o_ref):
    q0 = pl.program_id(2) * QSTEP
    cw = 2 * TQ

    def key_start(u):
        return pl.multiple_of(jnp.maximum(q0 + u * TQ - WINDOW, 0), TQ)

    def scores(ch):
        u, hf = ch
        k0 = key_start(u)
        cs = pl.multiple_of(WINDOW - (q0 + u * TQ - k0), LANES)
        return (_dot(k_ref[pl.ds(k0, WIN_KEYS), :], _chain_qt(q_ref, u, hf))
                + strip_ref[pl.ds(cs, WIN_KEYS), hf * cw:(hf + 1) * cw])

    def finish(ch, s):
        u, hf = ch
        e = jnp.exp2(s - jnp.max(s, axis=0, keepdims=True)).astype(BF16)
        o = _dot(vt_ref[:, pl.ds(key_start(u), WIN_KEYS)], e)
        o = o[:HEAD_DIM] * (1.0 / o[HEAD_DIM:HEAD_DIM + 1])
        _store_chain_out(o_ref, o, gate_ref, 2, u, hf)

    _run_chains(scores, finish)


def _nsa_win(qt, kwin, vwin_t, strip, gates, B, S):
    assert S >= WIN_KEYS
    return pl.pallas_call(
        _nsa_win_kernel,
        grid=(B, N_GROUPS, S // QSTEP),
        in_specs=[
            _Q_SPEC,
            pl.BlockSpec((None, None, S, LANES), lambda b, g, i: (b, g, 0, 0)),
            pl.BlockSpec((None, None, V_ROWS, S), lambda b, g, i: (b, g, 0, 0)),
            pl.BlockSpec((None, WIN_STRIP_W, ROWS), lambda b, g, i: (g, 0, 0)),
            _GATE_SPEC,
        ],
        out_specs=_O_SPEC,
        out_shape=jax.ShapeDtypeStruct((B, N_HEADS * HEAD_DIM, S), F32),
        compiler_params=_cparams(3),
    )(qt, kwin, vwin_t, strip, gates)


def _moba_kernel(q_ref, kmean_ref, k_ref, oh_ref, vt_ref, strip_ref, o_ref):
    q0 = pl.program_id(2) * QSTEP
    cw = 2 * TQ
    n = lax.broadcasted_iota(jnp.int32, (LANES, cw), 0)
    qas = {}
    for u, hf in _chains():
        cblk = (q0 + u * TQ) // MOBA_BLOCK
        qt = _chain_qt(q_ref, u, hf)
        gate = _dot(kmean_ref[...], qt)
        past = n < cblk
        chosen = _topk_mark_t(jnp.where(past, gate, NEG), MOBA_TOPK)
        mb = jnp.where(n == cblk, 0.0, jnp.where(past, jnp.where(chosen == -jnp.inf, 0.0, NEG), NEG)).astype(BF16)
        qas[(u, hf)] = jnp.concatenate([mb, qt], axis=0)
    n_tiles = q0 // TK + 1
    o = _flash_tiles(qas, k_ref, oh_ref, vt_ref, strip_ref, q0, n_tiles)
    for u, hf in _chains():
        _store_chain_out(o_ref, o[(u, hf)], None, 0, u, hf)


def _moba_attn(qt, kmean, k, vt, strip, B, S):
    n_blk = S // MOBA_BLOCK
    assert n_blk <= LANES
    onehot = (jnp.arange(S)[:, None] // MOBA_BLOCK == jnp.arange(LANES)[None, :]).astype(BF16)
    return pl.pallas_call(
        _moba_kernel,
        grid=(B, N_GROUPS, S // QSTEP),
        in_specs=[
            _Q_SPEC,
            pl.BlockSpec((None, None, LANES, LANES), lambda b, g, i: (b, g, 0, 0)),
            pl.BlockSpec((None, None, S, LANES), lambda b, g, i: (b, g, 0, 0)),
            pl.BlockSpec((S, LANES), lambda b, g, i: (0, 0)),
            pl.BlockSpec((None, None, V_ROWS, S), lambda b, g, i: (b, g, 0, 0)),
            pl.BlockSpec((None, STRIP_W, ROWS), lambda b, g, i: (g, 0, 0)),
        ],
        out_specs=_O_SPEC,
        out_shape=jax.ShapeDtypeStruct((B, N_HEADS * HEAD_DIM, S), F32),
        compiler_params=_cparams(3, _INTERLEAVE),
    )(qt, kmean, k, onehot, vt, strip)


def _out_proj_kernel(*refs):
    *o_refs, w_ref, h_ref, out_ref = refs
    o = o_refs[0][...]
    for r in o_refs[1:]:
        o = o + r[...]
    out_ref[...] = h_ref[...] + _dot(o.T.astype(BF16), w_ref[...])


def _out_proj(parts_t, w_out, h2, B, S):
    d = parts_t[0].shape[1]
    D = w_out.shape[1]
    tm = 512
    nt = S // tm
    return pl.pallas_call(
        _out_proj_kernel,
        grid=(B * nt,),
        in_specs=[pl.BlockSpec((None, d, tm), lambda i: (i // nt, 0, i % nt)) for _ in parts_t] + [
            pl.BlockSpec((d, D), lambda i: (0, 0)),
            pl.BlockSpec((tm, D), lambda i: (i, 0)),
        ],
        out_specs=pl.BlockSpec((tm, D), lambda i: (i, 0)),
        out_shape=jax.ShapeDtypeStruct((B * S, D), F32),
        compiler_params=_cparams(1),
    )(*parts_t, w_out.astype(BF16), h2)


def _mlp_kernel(h_ref, g_ref, wu_ref, wd_ref, out_ref, *, ff_chunk):
    h = h_ref[...]
    ms = jnp.mean(h * h, axis=-1, keepdims=True)
    xn = (h * lax.rsqrt(ms + RMS_EPS) * g_ref[...]).astype(BF16)
    acc = h
    for c in range(wu_ref.shape[1] // ff_chunk):
        a = jnp.maximum(_dot(xn, wu_ref[:, c * ff_chunk:(c + 1) * ff_chunk]), 0.0)
        acc = acc + _dot((a * a).astype(BF16), wd_ref[c * ff_chunk:(c + 1) * ff_chunk, :])
    out_ref[...] = acc


def _mlp(h2, gain, w_up, w_down):
    n, D = h2.shape
    F = w_up.shape[1]
    tm = 256
    return pl.pallas_call(
        functools.partial(_mlp_kernel, ff_chunk=1024),
        grid=(n // tm,),
        in_specs=[
            pl.BlockSpec((tm, D), lambda i: (i, 0)),
            pl.BlockSpec((1, D), lambda i: (0, 0)),
            pl.BlockSpec((D, F), lambda i: (0, 0)),
            pl.BlockSpec((F, D), lambda i: (0, 0)),
        ],
        out_specs=pl.BlockSpec((tm, D), lambda i: (i, 0)),
        out_shape=jax.ShapeDtypeStruct((n, D), F32),
        compiler_params=_cparams(1),
    )(h2, gain[None, :].astype(F32), w_up.astype(BF16), w_down.astype(BF16))


def _nsa_layer(h2, B, S, norm_mix, w_in, q_gain, k_gain, cmp_pos, cmp_w1, cmp_w2, w_out, sel_strip, win_strip):
    qt, cmp_raw, ksel, vsel_t, kwin, vwin_t, gates = _nsa_project(h2, norm_mix, w_in, q_gain, k_gain, B, S)
    kc, vct = _nsa_compress(cmp_raw, cmp_pos, cmp_w1, cmp_w2, k_gain[0], B, S)
    o_cmp, mb = _nsa_cmp(qt, kc, vct, gates, B, S)
    o_sel = _nsa_sel(qt, mb, ksel, vsel_t, sel_strip, gates, B, S)
    o_win = _nsa_win(qt, kwin, vwin_t, win_strip, gates, B, S)
    return _out_proj([o_cmp, o_sel, o_win], w_out, h2, B, S)


def _shared_kv_and_q(h2, B, S, norm_mix, kv_norm, kv_w, kv_k_gain, w_q, q_gain):
    qt, k, vt, kmean = _moba_project(h2, norm_mix, kv_norm, w_q, kv_w, q_gain, kv_k_gain, B, S)
    n_blk = S // MOBA_BLOCK
    km = kmean.reshape(B, n_blk, N_GROUPS, HEAD_DIM).transpose(0, 2, 1, 3)
    km = jnp.concatenate([km, km], axis=-1)
    km = jnp.pad(km, ((0, 0), (0, 0), (0, LANES - n_blk), (0, 0))).astype(BF16)
    return qt, k, vt, km


def kernel(x, norm_mix, norm_mlp, nsa_w_in, nsa_q_gain, nsa_k_gain, nsa_cmp_pos, nsa_cmp_w1, nsa_cmp_w2,
           nsa_w_out, kv_norm, kv_w, kv_k_gain, moba_w_q, moba_q_gain, moba_w_out, rel_table, mlp_w_up,
           mlp_w_down):
    B, S, D = x.shape
    depth = norm_mix.shape[0]
    n_a = nsa_w_in.shape[0]
    assert S % TK == 0 and S % MOBA_BLOCK == 0
    sel_strip = _bias_strip(rel_table, STRIP_W, DELTA_MAX, 1 << 30)
    win_strip = _bias_strip(rel_table, WIN_STRIP_W, WINDOW, WINDOW)
    h2 = x.reshape(B * S, D)
    shared = None
    for layer in range(depth):
        if layer < n_a:
            i = layer
            h2 = _nsa_layer(h2, B, S, norm_mix[layer], nsa_w_in[i], nsa_q_gain[i], nsa_k_gain[i],
                            nsa_cmp_pos[i], nsa_cmp_w1[i], nsa_cmp_w2[i], nsa_w_out[i], sel_strip, win_strip)
        else:
            j = layer - n_a
            if shared is None:
                qt, k, vt, km = _shared_kv_and_q(h2, B, S, norm_mix[layer], kv_norm, kv_w, kv_k_gain,
                                                 moba_w_q[j], moba_q_gain[j])
                shared = (k, vt, km)
            else:
                qt = _moba_project(h2, norm_mix[layer], kv_norm, moba_w_q[j], kv_w, moba_q_gain[j],
                                   kv_k_gain, B, S)[0]
            k, vt, km = shared
            o = _moba_attn(qt, km, k, vt, sel_strip, B, S)
            h2 = _out_proj([o], moba_w_out[j], h2, B, S)
        h2 = _mlp(h2, norm_mlp[layer], mlp_w_up[layer], mlp_w_down[layer])
    return h2.reshape(B, S, D)
```

```python
import functools
import math

import jax
import jax.numpy as jnp
from jax import lax
from jax.experimental import pallas as pl
from jax.experimental.pallas import tpu as pltpu

F32 = jnp.float32
BF16 = jnp.bfloat16

N_HEADS = 16
HEAD_DIM = 64
N_GROUPS = 4
R = N_HEADS // N_GROUPS
LANES = 128
CMP_BLOCK = 32
CMP_STRIDE = 16
CMP_HIDDEN = 4 * HEAD_DIM
SEL_BLOCK = 64
SEL_TOPN = 16
N_FORCED = 3
WINDOW = 512
MOBA_BLOCK = 256
MOBA_TOPK = 3
REL_BUCKETS = 32
REL_MAX_DIST = 4096
RMS_EPS = 1e-6
NEG = -1e30
BIG = 1e9
LOG2E = math.log2(math.e)
GATE_ROWS = 16
V_ROWS = HEAD_DIM + 16

TQ = 128
QSTEP = 512
NSUB = QSTEP // TQ
TK = 512
ROWS = R * TQ
QK_AHEAD = 4
FAR_DIST = 2897
DELTA_MAX = -(-(FAR_DIST + TK - 1) // LANES) * LANES
STRIP_W = -(-(DELTA_MAX + QSTEP) // 256) * 256
WIN_KEYS = WINDOW + TQ
WIN_STRIP_W = -(-(WINDOW + WIN_KEYS) // 256) * 256
VMEM_LIMIT = 56 * 1024 * 1024


def _cparams(n_axes):
    return pltpu.CompilerParams(dimension_semantics=("arbitrary",) * n_axes,
                                vmem_limit_bytes=VMEM_LIMIT)


def _dot(a, b):
    return jnp.dot(a, b, preferred_element_type=F32)


def _split3(x):
    parts = []
    rem = x
    for _ in range(3):
        hi = rem.astype(BF16)
        parts.append(hi)
        rem = rem - hi.astype(F32)
    return parts


def _strip_kernel(tab_ref, o_ref, *, c0, dlimit, ch):
    g = pl.program_id(0)
    j = pl.program_id(1)
    r = pl.program_id(2)
    h = g * R + r
    span = 2 * ch
    shape = (8, span)
    p = lax.broadcasted_iota(jnp.int32, shape, 1)
    d = jnp.where(p < ch, p, p - span) + c0 - j * ch
    n = jnp.maximum(d, 0)
    max_exact = REL_BUCKETS // 2
    nf = jnp.maximum(n, max_exact).astype(F32)
    large = max_exact + (jnp.log(nf / max_exact) / math.log(REL_MAX_DIST / max_exact)
                         * (REL_BUCKETS - max_exact)).astype(jnp.int32)
    large = jnp.minimum(large, REL_BUCKETS - 1)
    bucket = jnp.where(n < max_exact, n, large)
    val = jnp.zeros(shape, F32)
    for b in range(REL_BUCKETS):
        val = jnp.where(bucket == b, tab_ref[b, h], val)
    val = val - tab_ref[REL_BUCKETS - 1, h]
    ok = jnp.where(d >= 0, jnp.where(d < dlimit, 1, 0), 0)
    w = jnp.where(ok > 0, val * LOG2E, NEG)
    wb = jnp.broadcast_to(w[0:1, :], (ch, span))
    o_ref[...] = pltpu.roll(wb, 0, 1, stride=1, stride_axis=0)[:, :TQ]


def _bias_strip(rel_table, width, c0, dlimit):
    ch = 256
    return pl.pallas_call(
        functools.partial(_strip_kernel, c0=c0, dlimit=dlimit, ch=ch),
        grid=(N_GROUPS, width // ch, R),
        in_specs=[pl.BlockSpec(memory_space=pltpu.SMEM)],
        out_specs=pl.BlockSpec((None, ch, TQ), lambda g, j, r: (g, j, r)),
        out_shape=jax.ShapeDtypeStruct((N_GROUPS, width, ROWS), F32),
        compiler_params=_cparams(3),
    )(rel_table.astype(F32))


def _head_norm(y, bd_ref, gain):
    outs = []
    for c in range(y.shape[1] // LANES):
        yc = y[:, c * LANES:(c + 1) * LANES]
        y2 = yc * yc
        hi = y2.astype(BF16)
        lo = (y2 - hi.astype(F32)).astype(BF16)
        ss = _dot(hi, bd_ref[...]) + _dot(lo, bd_ref[...])
        outs.append(yc * lax.rsqrt(ss * (1.0 / HEAD_DIM) + RMS_EPS))
    return jnp.concatenate(outs, axis=1) * gain


def _store_heads_t(q_ref, y):
    yt = y.T
    row = lax.broadcasted_iota(jnp.int32, (LANES, y.shape[0]), 0)
    for h in range(N_HEADS):
        pair = yt[(h // 2) * LANES:(h // 2 + 1) * LANES, :]
        keep = row < HEAD_DIM if h % 2 == 0 else row >= HEAD_DIM
        q_ref[h] = jnp.where(keep, pair, 0.0).astype(q_ref.dtype)


def _store_groups_dup(o_ref, y):
    lane = lax.broadcasted_iota(jnp.int32, (y.shape[0], LANES), 1)
    for g in range(N_GROUPS):
        pair = y[:, (g // 2) * LANES:(g // 2 + 1) * LANES]
        keep = lane < HEAD_DIM if g % 2 == 0 else lane >= HEAD_DIM
        m = jnp.where(keep, pair, 0.0)
        o_ref[g] = (m + pltpu.roll(m, HEAD_DIM, 1)).astype(o_ref.dtype)


def _store_groups_t(o_ref, y):
    yt = y.T
    ones = jnp.ones((V_ROWS - HEAD_DIM, y.shape[0]), o_ref.dtype)
    for g in range(N_GROUPS):
        o_ref[g, 0:HEAD_DIM, :] = yt[g * HEAD_DIM:(g + 1) * HEAD_DIM, :].astype(o_ref.dtype)
        o_ref[g, HEAD_DIM:V_ROWS, :] = ones


def _nsa_proj_kernel(x_ref, gn_ref, w_ref, bd_ref, qg_ref, kg_ref,
                     q_ref, cmp_ref, ksel_ref, vsel_ref, kwin_ref, vwin_ref, gate_ref):
    x = x_ref[...]
    ms = jnp.mean(x * x, axis=-1, keepdims=True)
    xn = (x * lax.rsqrt(ms + RMS_EPS) * gn_ref[...]).astype(BF16)
    d = N_HEADS * HEAD_DIM
    gd = N_GROUPS * HEAD_DIM
    yq = _dot(xn, w_ref[:, 0:d])
    _store_heads_t(q_ref, _head_norm(yq, bd_ref, qg_ref[...]))
    ycmp = _dot(xn, w_ref[:, d:d + 2 * gd])
    for c in range(2 * gd // LANES):
        cmp_ref[c] = ycmp[:, c * LANES:(c + 1) * LANES]
    ysel = _dot(xn, w_ref[:, d + 2 * gd:d + 4 * gd])
    _store_groups_dup(ksel_ref, _head_norm(ysel[:, :gd], bd_ref, kg_ref[0:1, :]))
    _store_groups_t(vsel_ref, ysel[:, gd:])
    ywin = _dot(xn, w_ref[:, d + 4 * gd:d + 6 * gd])
    _store_groups_dup(kwin_ref, _head_norm(ywin[:, :gd], bd_ref, kg_ref[1:2, :]))
    _store_groups_t(vwin_ref, ywin[:, gd:])
    yg = _dot(xn, w_ref[:, d + 6 * gd:])
    sgt = (1.0 / (1.0 + jnp.exp(-yg))).T
    for g in range(N_GROUPS):
        gate_ref[g] = sgt[g * LANES:g * LANES + GATE_ROWS, :]


def _qk_side_inputs(q_gain):
    bd = jnp.kron(jnp.eye(2, dtype=F32), jnp.ones((HEAD_DIM, HEAD_DIM), F32)).astype(BF16)
    qg = jnp.tile(q_gain.astype(F32) * (HEAD_DIM ** -0.5 * LOG2E), N_HEADS)[None, :]
    return bd, qg


def _nsa_project(x2, gain, w_in, q_gain, k_gain, B, S):
    D = x2.shape[1]
    d = N_HEADS * HEAD_DIM
    gd = N_GROUPS * HEAD_DIM
    wg = w_in[:, d + 6 * gd:].reshape(D, N_GROUPS, 3 * R)
    wg = jnp.pad(wg, ((0, 0), (0, 0), (0, LANES - 3 * R))).reshape(D, N_GROUPS * LANES)
    w = jnp.concatenate([w_in[:, :d + 6 * gd], wg], axis=1).astype(BF16)
    ncol = w.shape[1]
    bd, qg = _qk_side_inputs(q_gain)
    kg = jnp.stack([jnp.tile(k_gain[1].astype(F32), N_GROUPS), jnp.tile(k_gain[2].astype(F32), N_GROUPS)])
    tm = 256
    nt = S // tm
    kspec = pl.BlockSpec((None, N_GROUPS, tm, LANES), lambda i: (i // nt, 0, i % nt, 0))
    vspec = pl.BlockSpec((None, N_GROUPS, V_ROWS, tm), lambda i: (i // nt, 0, 0, i % nt))
    kshape = jax.ShapeDtypeStruct((B, N_GROUPS, S, LANES), BF16)
    vshape = jax.ShapeDtypeStruct((B, N_GROUPS, V_ROWS, S), BF16)
    return pl.pallas_call(
        _nsa_proj_kernel,
        grid=(B * nt,),
        in_specs=[
            pl.BlockSpec((tm, D), lambda i: (i, 0)),
            pl.BlockSpec((1, D), lambda i: (0, 0)),
            pl.BlockSpec((D, ncol), lambda i: (0, 0)),
            pl.BlockSpec((LANES, LANES), lambda i: (0, 0)),
            pl.BlockSpec((1, d), lambda i: (0, 0)),
            pl.BlockSpec((2, gd), lambda i: (0, 0)),
        ],
        out_specs=[
            pl.BlockSpec((None, N_HEADS, LANES, tm), lambda i: (i // nt, 0, 0, i % nt)),
            pl.BlockSpec((2 * gd // LANES, tm, LANES), lambda i: (0, i, 0)),
            kspec, vspec, kspec, vspec,
            pl.BlockSpec((None, N_GROUPS, GATE_ROWS, tm), lambda i: (i // nt, 0, 0, i % nt)),
        ],
        out_shape=[
            jax.ShapeDtypeStruct((B, N_HEADS, LANES, S), BF16),
            jax.ShapeDtypeStruct((2 * gd // LANES, B * S, LANES), F32),
            kshape, vshape, kshape, vshape,
            jax.ShapeDtypeStruct((B, N_GROUPS, GATE_ROWS, S), F32),
        ],
        compiler_params=_cparams(1),
    )(x2, gain[None, :].astype(F32), w, bd, qg, kg)


def _moba_proj_kernel(x_ref, gq_ref, gkv_ref, wq_ref, wkv_ref, bd_ref, qg_ref, kg_ref,
                      q_ref, k_ref, v_ref, kmean_ref):
    x = x_ref[...]
    ms = jnp.mean(x * x, axis=-1, keepdims=True)
    xr = x * lax.rsqrt(ms + RMS_EPS)
    gd = N_GROUPS * HEAD_DIM
    yq = _dot((xr * gq_ref[...]).astype(BF16), wq_ref[...])
    _store_heads_t(q_ref, _head_norm(yq, bd_ref, qg_ref[...]))
    ykv = _dot((xr * gkv_ref[...]).astype(BF16), wkv_ref[...])
    kn = _head_norm(ykv[:, :gd], bd_ref, kg_ref[...])
    _store_groups_dup(k_ref, kn)
    _store_groups_t(v_ref, ykv[:, gd:])
    kmean_ref[...] = jnp.mean(kn, axis=0, keepdims=True)


def _moba_project(h2, g_mix, g_kv, w_q, kv_w, q_gain, k_gain, B, S):
    D = h2.shape[1]
    d = N_HEADS * HEAD_DIM
    gd = N_GROUPS * HEAD_DIM
    bd, qg = _qk_side_inputs(q_gain)
    kg = jnp.tile(k_gain.astype(F32), N_GROUPS)[None, :]
    tm = MOBA_BLOCK
    nt = S // tm
    return pl.pallas_call(
        _moba_proj_kernel,
        grid=(B * nt,),
        in_specs=[
            pl.BlockSpec((tm, D), lambda i: (i, 0)),
            pl.BlockSpec((1, D), lambda i: (0, 0)),
            pl.BlockSpec((1, D), lambda i: (0, 0)),
            pl.BlockSpec((D, d), lambda i: (0, 0)),
            pl.BlockSpec((D, 2 * gd), lambda i: (0, 0)),
            pl.BlockSpec((LANES, LANES), lambda i: (0, 0)),
            pl.BlockSpec((1, d), lambda i: (0, 0)),
            pl.BlockSpec((1, gd), lambda i: (0, 0)),
        ],
        out_specs=[
            pl.BlockSpec((None, N_HEADS, LANES, tm), lambda i: (i // nt, 0, 0, i % nt)),
            pl.BlockSpec((None, N_GROUPS, tm, LANES), lambda i: (i // nt, 0, i % nt, 0)),
            pl.BlockSpec((None, N_GROUPS, V_ROWS, tm), lambda i: (i // nt, 0, 0, i % nt)),
            pl.BlockSpec((None, 1, gd), lambda i: (i, 0, 0)),
        ],
        out_shape=[
            jax.ShapeDtypeStruct((B, N_HEADS, LANES, S), BF16),
            jax.ShapeDtypeStruct((B, N_GROUPS, S, LANES), BF16),
            jax.ShapeDtypeStruct((B, N_GROUPS, V_ROWS, S), BF16),
            jax.ShapeDtypeStruct((B * nt, 1, gd), F32),
        ],
        compiler_params=_cparams(1),
    )(h2, g_mix[None, :].astype(F32), g_kv[None, :].astype(F32), w_q.astype(BF16), kv_w.astype(BF16),
      bd, qg, kg)


def _compress_kernel(x_ref, pos_ref, w1_ref, w2_ref, kg_ref, kc_ref, vct_ref, *, n_chunk):
    t = pl.program_id(1)
    half = CMP_BLOCK // 2
    za = [[], []]
    zb = [[], []]
    for l in range(half):
        for p in range(2):
            xl = x_ref[p, pl.ds(l, n_chunk, stride=CMP_STRIDE), :]
            za[p].append((xl + pos_ref[l:l + 1, p * LANES:(p + 1) * LANES]).astype(BF16))
            zb[p].append((xl + pos_ref[half + l:half + l + 1, p * LANES:(p + 1) * LANES]).astype(BF16))
    za = [jnp.concatenate(z, axis=1) for z in za]
    zb = [jnp.concatenate(z, axis=1) for z in zb]
    for g in range(N_GROUPS):
        p, e = g // 2, g % 2
        first = _dot(za[p], w1_ref[0, e])
        second = _dot(zb[p], w1_ref[1, e])
        hid = first + pltpu.roll(second, n_chunk - 1, 0)
        hid = hid * (1.0 / (1.0 + jnp.exp(-hid)))
        out = _dot(hid.astype(BF16), w2_ref[...])

        @pl.when(t == 0)
        def _():
            normed = out * lax.rsqrt(jnp.mean(out * out, axis=-1, keepdims=True) + RMS_EPS) * kg_ref[...]
            kc_ref[g] = normed.astype(kc_ref.dtype)

        @pl.when(t == 1)
        def _():
            vct_ref[g] = out.T[:HEAD_DIM, :].astype(vct_ref.dtype)


def _nsa_compress(cmp_raw, cmp_pos, cmp_w1, cmp_w2, k_gain0, B, S):
    gd = N_GROUPS * HEAD_DIM
    half = CMP_BLOCK // 2
    n_chunk = S // CMP_STRIDE
    pos = jnp.tile(cmp_pos.astype(F32), (1, 1, N_GROUPS))
    w1 = cmp_w1.astype(F32).reshape(2, 2, half, HEAD_DIM, CMP_HIDDEN)
    z = jnp.zeros_like(w1)
    w1p = jnp.stack([jnp.concatenate([w1, z], axis=3), jnp.concatenate([z, w1], axis=3)], axis=2)
    w1p = w1p.reshape(2, 2, 2, half * LANES, CMP_HIDDEN).astype(BF16)
    w2 = jnp.concatenate([cmp_w2, cmp_w2], axis=-1).astype(BF16)
    kg = jnp.tile(k_gain0.astype(F32), 2)[None, :]
    return pl.pallas_call(
        functools.partial(_compress_kernel, n_chunk=n_chunk),
        grid=(B, 2),
        in_specs=[
            pl.BlockSpec((2, S, LANES), lambda b, t: (t, b, 0)),
            pl.BlockSpec((None, CMP_BLOCK, gd), lambda b, t: (t, 0, 0)),
            pl.BlockSpec((None, 2, 2, half * LANES, CMP_HIDDEN), lambda b, t: (t, 0, 0, 0, 0)),
            pl.BlockSpec((None, CMP_HIDDEN, LANES), lambda b, t: (t, 0, 0)),
            pl.BlockSpec((1, LANES), lambda b, t: (0, 0)),
        ],
        out_specs=[
            pl.BlockSpec((None, N_GROUPS, n_chunk, LANES), lambda b, t: (b, 0, 0, 0)),
            pl.BlockSpec((None, N_GROUPS, HEAD_DIM, n_chunk), lambda b, t: (b, 0, 0, 0)),
        ],
        out_shape=[
            jax.ShapeDtypeStruct((B, N_GROUPS, n_chunk, LANES), BF16),
            jax.ShapeDtypeStruct((B, N_GROUPS, HEAD_DIM, n_chunk), BF16),
        ],
        compiler_params=_cparams(2),
    )(cmp_raw, pos, w1p, w2, kg)


_Q_SPEC = pl.BlockSpec((None, R, LANES, QSTEP), lambda b, g, i: (b, g, 0, i))
_GATE_SPEC = pl.BlockSpec((None, None, GATE_ROWS, QSTEP), lambda b, g, i: (b, g, 0, i))
_O_SPEC = pl.BlockSpec((None, R * HEAD_DIM, QSTEP), lambda b, g, i: (b, g, i))


def _chains():
    return [(u, hf) for u in range(NSUB) for hf in range(R // 2)]


def _run_chains(scores, finish, ahead=None, next_scores=None):
    chains = _chains()
    ahead = [scores(ch) for ch in chains[:QK_AHEAD]] if ahead is None else list(ahead)
    for idx, ch in enumerate(chains):
        s = ahead.pop(0)
        nxt = idx + QK_AHEAD
        if nxt < len(chains):
            ahead.append(scores(chains[nxt]))
        elif next_scores is not None:
            ahead.append(next_scores(chains[nxt - len(chains)]))
        finish(ch, s)
    return ahead


def _chain_qt(q_ref, u, hf):
    return jnp.concatenate([q_ref[2 * hf + e, :, u * TQ:(u + 1) * TQ] for e in range(2)], axis=1)


def _store_chain_out(o_ref, ot, gate_ref, gate_col, u, hf):
    for e in range(2):
        r = 2 * hf + e
        o_r = ot[:, e * TQ:(e + 1) * TQ]
        if gate_ref is not None:
            c = 3 * r + gate_col
            o_r = o_r * gate_ref[c:c + 1, u * TQ:(u + 1) * TQ]
        o_ref[r * HEAD_DIM:(r + 1) * HEAD_DIM, u * TQ:(u + 1) * TQ] = o_r


def _topk_mark_t(score, k):
    row = lax.broadcasted_iota(jnp.int32, score.shape, 0).astype(F32)
    s = score
    for _ in range(k):
        m = jnp.max(s, axis=0, keepdims=True)
        idx = jnp.min(jnp.where(s == m, row, float(LANES)), axis=0, keepdims=True)
        s = jnp.where(row == idx, -jnp.inf, s)
    return s


_FLASH_SCRATCH = [pltpu.VMEM((QK_AHEAD, TK, 2 * TQ), F32), pltpu.VMEM((QK_AHEAD, 8, 2 * TQ), F32)]


def _flash_tiles(qas, k_ref, oh_ref, vt_ref, strip_ref, s_scr, mx_scr, q0, n_tiles):
    chains = _chains()
    cw = 2 * TQ
    last_k0 = k_ref.shape[0] - TK

    def tile_scores(kt):
        k0 = pl.multiple_of(jnp.minimum(kt * TK, last_k0), TK)
        ka = jnp.concatenate([oh_ref[pl.ds(k0, TK), :], k_ref[pl.ds(k0, TK), :]], axis=1)

        def scores(ch):
            u, hf = ch
            cs = pl.multiple_of(DELTA_MAX - jnp.clip(q0 + u * TQ - k0, 0, DELTA_MAX), LANES)
            s = _dot(ka, qas[ch]) + strip_ref[pl.ds(cs, TK), hf * cw:(hf + 1) * cw]
            return s, jnp.max(s, axis=0, keepdims=True)

        return scores

    def park(slot, s_and_max):
        s_scr[slot] = s_and_max[0]
        mx_scr[slot] = jnp.broadcast_to(s_and_max[1], mx_scr.shape[1:])

    def body(kt, state):
        vt = vt_ref[:, pl.ds(pl.multiple_of(kt * TK, TK), TK)]
        cur, nxt = tile_scores(kt), tile_scores(kt + 1)
        in_flight = {}
        out = []
        for idx, ch in enumerate(chains):
            s, s_max = (s_scr[idx], mx_scr[idx, 0:1]) if idx < QK_AHEAD else in_flight.pop(idx)
            ahead = idx + QK_AHEAD
            if ahead < len(chains):
                in_flight[ahead] = cur(chains[ahead])
            else:
                park(ahead - len(chains), nxt(chains[ahead - len(chains)]))
            m, acc = state[idx]
            m_new = jnp.maximum(m, s_max)
            p = jnp.exp2(s - m_new).astype(BF16)
            out.append((m_new, jnp.exp2(m - m_new) * acc + _dot(vt, p)))
        return tuple(out)

    first = tile_scores(0)
    for slot in range(QK_AHEAD):
        park(slot, first(chains[slot]))
    init = tuple((jnp.full((1, cw), 3 * NEG, F32), jnp.zeros((V_ROWS, cw), F32)) for _ in chains)
    fin = lax.fori_loop(0, n_tiles, body, init)
    return {ch: acc[:HEAD_DIM] * (1.0 / acc[HEAD_DIM:HEAD_DIM + 1]) for ch, (m, acc) in zip(chains, fin)}


def _nsa_cmp_kernel(q_ref, kc_ref, vct_ref, vis_ref, c2st_ref, gate_ref, o_ref, mb_ref, imp_scr, *, n_chunk):
    q0 = pl.program_id(2) * QSTEP
    cw = 2 * TQ
    i_col = lax.broadcasted_iota(jnp.int32, (1, cw), 1) & (TQ - 1)

    def attend(n_used):
        psums = [None] * NSUB

        def scores(ch):
            r0 = pl.multiple_of(n_chunk - (q0 + ch[0] * TQ) // CMP_STRIDE, 8)
            return _dot(kc_ref[0:n_used, :], _chain_qt(q_ref, *ch)) + vis_ref[pl.ds(r0, n_used), :]

        def finish(ch, lcm):
            u, hf = ch
            e = jnp.exp2(lcm - jnp.max(lcm, axis=0, keepdims=True))
            any_visible = q0 + u * TQ + i_col >= CMP_BLOCK - 1
            pc = e * jnp.where(any_visible, 1.0 / jnp.sum(e, axis=0, keepdims=True), 0.0)
            oc = _dot(vct_ref[:, 0:n_used], pc.astype(BF16))
            _store_chain_out(o_ref, oc, gate_ref, 0, u, hf)
            ps = pc[:, :TQ] + pc[:, TQ:]
            psums[u] = ps if psums[u] is None else psums[u] + ps

        _run_chains(scores, finish)
        psum = jnp.concatenate(psums, axis=1)
        imp = None
        for part in _split3(psum):
            term = _dot(c2st_ref[:, 0:n_used], part)
            imp = term if imp is None else imp + term
        imp_scr[...] = imp

    span = LANES * CMP_STRIDE
    for v in range(1, n_chunk // LANES + 1):
        pl.when(q0 // span + 1 == v)(functools.partial(attend, LANES * v))

    imp = imp_scr[...]
    cur = (q0 + lax.broadcasted_iota(jnp.int32, (LANES, QSTEP), 1)) // SEL_BLOCK
    j = lax.broadcasted_iota(jnp.int32, (LANES, QSTEP), 0)
    forced_or_imp = jnp.where(j == 0, -jnp.inf, jnp.where(j >= cur - 1, -jnp.inf, imp))
    score = jnp.where(j <= cur, forced_or_imp, -BIG)
    chosen = _topk_mark_t(score, SEL_TOPN - N_FORCED)
    mb_ref[...] = jnp.where(chosen == -jnp.inf, 0.0, NEG).astype(mb_ref.dtype)


def _nsa_cmp(qt, kc, vct, gates, B, S):
    n_chunk = S // CMP_STRIDE
    n_sel = S // SEL_BLOCK
    assert n_sel <= LANES and n_chunk % LANES == 0
    cs = jnp.arange(n_chunk)[None, :] * CMP_STRIDE
    ss = jnp.arange(LANES)[:, None] * SEL_BLOCK
    overlap = jnp.clip(jnp.minimum(cs + CMP_BLOCK, ss + SEL_BLOCK) - jnp.maximum(cs, ss), 0, None) / CMP_BLOCK
    n_cmp = (S - CMP_BLOCK) // CMP_STRIDE + 1
    c2st = jnp.where((jnp.arange(n_chunk)[None, :] < n_cmp) & (jnp.arange(LANES)[:, None] < n_sel),
                     overlap, 0.0).astype(BF16)
    c_end = (jnp.arange(2 * n_chunk)[:, None] - n_chunk) * CMP_STRIDE + (CMP_BLOCK - 1)
    vis = jnp.where(c_end <= (jnp.arange(2 * TQ)[None, :] & (TQ - 1)), 0.0, NEG).astype(F32)
    return pl.pallas_call(
        functools.partial(_nsa_cmp_kernel, n_chunk=n_chunk),
        grid=(B, N_GROUPS, S // QSTEP),
        in_specs=[
            _Q_SPEC,
            pl.BlockSpec((None, None, n_chunk, LANES), lambda b, g, i: (b, g, 0, 0)),
            pl.BlockSpec((None, None, HEAD_DIM, n_chunk), lambda b, g, i: (b, g, 0, 0)),
            pl.BlockSpec((2 * n_chunk, 2 * TQ), lambda b, g, i: (0, 0)),
            pl.BlockSpec((LANES, n_chunk), lambda b, g, i: (0, 0)),
            _GATE_SPEC,
        ],
        out_specs=[
            _O_SPEC,
            pl.BlockSpec((None, None, LANES, QSTEP), lambda b, g, i: (b, g, 0, i)),
        ],
        out_shape=[
            jax.ShapeDtypeStruct((B, N_HEADS * HEAD_DIM, S), F32),
            jax.ShapeDtypeStruct((B, N_GROUPS, LANES, S), BF16),
        ],
        scratch_shapes=[pltpu.VMEM((LANES, QSTEP), F32)],
        compiler_params=_cparams(3),
    )(qt, kc, vct, vis, c2st, gates)


def _nsa_sel_kernel(q_ref, mb_ref, k_ref, oh_ref, vt_ref, strip_ref, gate_ref, o_ref, s_scr, mx_scr):
    q0 = pl.program_id(2) * QSTEP
    qas = {}
    for u, hf in _chains():
        mb = mb_ref[:, u * TQ:(u + 1) * TQ]
        qas[(u, hf)] = jnp.concatenate([jnp.concatenate([mb, mb], axis=1), _chain_qt(q_ref, u, hf)], axis=0)
    n_tiles = (q0 + QSTEP) // TK
    o = _flash_tiles(qas, k_ref, oh_ref, vt_ref, strip_ref, s_scr, mx_scr, q0, n_tiles)
    for u, hf in _chains():
        _store_chain_out(o_ref, o[(u, hf)], gate_ref, 1, u, hf)


def _nsa_sel(qt, mb, ksel, vsel_t, strip, gates, B, S):
    onehot = (jnp.arange(S)[:, None] // SEL_BLOCK == jnp.arange(LANES)[None, :]).astype(BF16)
    return pl.pallas_call(
        _nsa_sel_kernel,
        grid=(B, N_GROUPS, S // QSTEP),
        in_specs=[
            _Q_SPEC,
            pl.BlockSpec((None, None, LANES, QSTEP), lambda b, g, i: (b, g, 0, i)),
            pl.BlockSpec((None, None, S, LANES), lambda b, g, i: (b, g, 0, 0)),
            pl.BlockSpec((S, LANES), lambda b, g, i: (0, 0)),
            pl.BlockSpec((None, None, V_ROWS, S), lambda b, g, i: (b, g, 0, 0)),
            pl.BlockSpec((None, STRIP_W, ROWS), lambda b, g, i: (g, 0, 0)),
            _GATE_SPEC,
        ],
        out_specs=_O_SPEC,
        out_shape=jax.ShapeDtypeStruct((B, N_HEADS * HEAD_DIM, S), F32),
        scratch_shapes=_FLASH_SCRATCH,
        compiler_params=_cparams(3),
    )(qt, mb, ksel, onehot, vsel_t, strip, gates)


def _nsa_win_kernel(q_ref, k_ref, vt_ref, strip_ref, gate_ref, o_ref):
    q0 = pl.program_id(2) * QSTEP
    cw = 2 * TQ

    def key_start(u):
        return pl.multiple_of(jnp.maximum(q0 + u * TQ - WINDOW, 0), TQ)

    def scores(ch):
        u, hf = ch
        k0 = key_start(u)
        cs = pl.multiple_of(WINDOW - (q0 + u * TQ - k0), LANES)
        return (_dot(k_ref[pl.ds(k0, WIN_KEYS), :], _chain_qt(q_ref, u, hf))
                + strip_ref[pl.ds(cs, WIN_KEYS), hf * cw:(hf + 1) * cw])

    def finish(ch, s):
        u, hf = ch
        e = jnp.exp2(s - jnp.max(s, axis=0, keepdims=True)).astype(BF16)
        o = _dot(vt_ref[:, pl.ds(key_start(u), WIN_KEYS)], e)
        o = o[:HEAD_DIM] * (1.0 / o[HEAD_DIM:HEAD_DIM + 1])
        _store_chain_out(o_ref, o, gate_ref, 2, u, hf)

    _run_chains(scores, finish)


def _nsa_win(qt, kwin, vwin_t, strip, gates, B, S):
    assert S >= WIN_KEYS
    return pl.pallas_call(
        _nsa_win_kernel,
        grid=(B, N_GROUPS, S // QSTEP),
        in_specs=[
            _Q_SPEC,
            pl.BlockSpec((None, None, S, LANES), lambda b, g, i: (b, g, 0, 0)),
            pl.BlockSpec((None, None, V_ROWS, S), lambda b, g, i: (b, g, 0, 0)),
            pl.BlockSpec((None, WIN_STRIP_W, ROWS), lambda b, g, i: (g, 0, 0)),
            _GATE_SPEC,
        ],
        out_specs=_O_SPEC,
        out_shape=jax.ShapeDtypeStruct((B, N_HEADS * HEAD_DIM, S), F32),
        compiler_params=_cparams(3),
    )(qt, kwin, vwin_t, strip, gates)


def _moba_kernel(q_ref, kmean_ref, k_ref, oh_ref, vt_ref, strip_ref, o_ref, s_scr, mx_scr):
    q0 = pl.program_id(2) * QSTEP
    cw = 2 * TQ
    nb = kmean_ref.shape[0]
    n = lax.broadcasted_iota(jnp.int32, (nb, cw), 0)
    no_block = jnp.full((LANES - nb, cw), NEG, BF16)
    qas = {}
    for u, hf in _chains():
        cblk = (q0 + u * TQ) // MOBA_BLOCK
        qt = _chain_qt(q_ref, u, hf)
        gate = _dot(kmean_ref[...], qt)
        past = n < cblk
        chosen = _topk_mark_t(jnp.where(past, gate, NEG), MOBA_TOPK)
        mb = jnp.where(n == cblk, 0.0, jnp.where(past, jnp.where(chosen == -jnp.inf, 0.0, NEG), NEG)).astype(BF16)
        qas[(u, hf)] = jnp.concatenate([mb, no_block, qt], axis=0)
    n_tiles = (q0 + QSTEP) // TK
    o = _flash_tiles(qas, k_ref, oh_ref, vt_ref, strip_ref, s_scr, mx_scr, q0, n_tiles)
    for u, hf in _chains():
        _store_chain_out(o_ref, o[(u, hf)], None, 0, u, hf)


def _moba_attn(qt, kmean, k, vt, strip, B, S):
    n_blk = S // MOBA_BLOCK
    assert n_blk <= LANES
    onehot = (jnp.arange(S)[:, None] // MOBA_BLOCK == jnp.arange(LANES)[None, :]).astype(BF16)
    return pl.pallas_call(
        _moba_kernel,
        grid=(B, N_GROUPS, S // QSTEP),
        in_specs=[
            _Q_SPEC,
            pl.BlockSpec((None, None, kmean.shape[2], LANES), lambda b, g, i: (b, g, 0, 0)),
            pl.BlockSpec((None, None, S, LANES), lambda b, g, i: (b, g, 0, 0)),
            pl.BlockSpec((S, LANES), lambda b, g, i: (0, 0)),
            pl.BlockSpec((None, None, V_ROWS, S), lambda b, g, i: (b, g, 0, 0)),
            pl.BlockSpec((None, STRIP_W, ROWS), lambda b, g, i: (g, 0, 0)),
        ],
        out_specs=_O_SPEC,
        out_shape=jax.ShapeDtypeStruct((B, N_HEADS * HEAD_DIM, S), F32),
        scratch_shapes=_FLASH_SCRATCH,
        compiler_params=_cparams(3),
    )(qt, kmean, k, onehot, vt, strip)


def _out_proj_kernel(*refs):
    *o_refs, w_ref, h_ref, out_ref = refs
    o = o_refs[0][...]
    for r in o_refs[1:]:
        o = o + r[...]
    out_ref[...] = h_ref[...] + lax.dot_general(o.astype(BF16), w_ref[...], (((0,), (0,)), ((), ())),
                                                 preferred_element_type=F32)


def _out_proj(parts_t, w_out, h2, B, S):
    d = parts_t[0].shape[1]
    D = w_out.shape[1]
    tm = 512
    nt = S // tm
    return pl.pallas_call(
        _out_proj_kernel,
        grid=(B * nt,),
        in_specs=[pl.BlockSpec((None, d, tm), lambda i: (i // nt, 0, i % nt)) for _ in parts_t] + [
            pl.BlockSpec((d, D), lambda i: (0, 0)),
            pl.BlockSpec((tm, D), lambda i: (i, 0)),
        ],
        out_specs=pl.BlockSpec((tm, D), lambda i: (i, 0)),
        out_shape=jax.ShapeDtypeStruct((B * S, D), F32),
        compiler_params=_cparams(1),
    )(*parts_t, w_out.astype(BF16), h2)


def _mlp_kernel(h_ref, g_ref, wu_ref, wd_ref, out_ref, *, ff_chunk):
    h = h_ref[...]
    ms = jnp.mean(h * h, axis=-1, keepdims=True)
    xn = (h * lax.rsqrt(ms + RMS_EPS) * g_ref[...]).astype(BF16)
    acc = h
    for c in range(wu_ref.shape[1] // ff_chunk):
        a = jnp.maximum(_dot(xn, wu_ref[:, c * ff_chunk:(c + 1) * ff_chunk]), 0.0)
        acc = acc + _dot((a * a).astype(BF16), wd_ref[c * ff_chunk:(c + 1) * ff_chunk, :])
    out_ref[...] = acc


def _mlp(h2, gain, w_up, w_down):
    n, D = h2.shape
    F = w_up.shape[1]
    tm = 256
    return pl.pallas_call(
        functools.partial(_mlp_kernel, ff_chunk=1024),
        grid=(n // tm,),
        in_specs=[
            pl.BlockSpec((tm, D), lambda i: (i, 0)),
            pl.BlockSpec((1, D), lambda i: (0, 0)),
            pl.BlockSpec((D, F), lambda i: (0, 0)),
            pl.BlockSpec((F, D), lambda i: (0, 0)),
        ],
        out_specs=pl.BlockSpec((tm, D), lambda i: (i, 0)),
        out_shape=jax.ShapeDtypeStruct((n, D), F32),
        compiler_params=_cparams(1),
    )(h2, gain[None, :].astype(F32), w_up.astype(BF16), w_down.astype(BF16))


def _nsa_layer(h2, B, S, norm_mix, w_in, q_gain, k_gain, cmp_pos, cmp_w1, cmp_w2, w_out, sel_strip, win_strip):
    qt, cmp_raw, ksel, vsel_t, kwin, vwin_t, gates = _nsa_project(h2, norm_mix, w_in, q_gain, k_gain, B, S)
    kc, vct = _nsa_compress(cmp_raw, cmp_pos, cmp_w1, cmp_w2, k_gain[0], B, S)
    o_cmp, mb = _nsa_cmp(qt, kc, vct, gates, B, S)
    o_sel = _nsa_sel(qt, mb, ksel, vsel_t, sel_strip, gates, B, S)
    o_win = _nsa_win(qt, kwin, vwin_t, win_strip, gates, B, S)
    return _out_proj([o_cmp, o_sel, o_win], w_out, h2, B, S)


def _shared_kv_and_q(h2, B, S, norm_mix, kv_norm, kv_w, kv_k_gain, w_q, q_gain):
    qt, k, vt, kmean = _moba_project(h2, norm_mix, kv_norm, w_q, kv_w, q_gain, kv_k_gain, B, S)
    n_blk = S // MOBA_BLOCK
    km = kmean.reshape(B, n_blk, N_GROUPS, HEAD_DIM).transpose(0, 2, 1, 3)
    km = jnp.concatenate([km, km], axis=-1)
    km = jnp.pad(km, ((0, 0), (0, 0), (0, -n_blk % 16), (0, 0))).astype(BF16)
    return qt, k, vt, km


def kernel(x, norm_mix, norm_mlp, nsa_w_in, nsa_q_gain, nsa_k_gain, nsa_cmp_pos, nsa_cmp_w1, nsa_cmp_w2,
           nsa_w_out, kv_norm, kv_w, kv_k_gain, moba_w_q, moba_q_gain, moba_w_out, rel_table, mlp_w_up,
           mlp_w_down):
    B, S, D = x.shape
    depth = norm_mix.shape[0]
    n_a = nsa_w_in.shape[0]
    assert S % TK == 0 and S % MOBA_BLOCK == 0
    sel_strip = _bias_strip(rel_table, STRIP_W, DELTA_MAX, 1 << 30)
    win_strip = _bias_strip(rel_table, WIN_STRIP_W, WINDOW, WINDOW)
    h2 = x.reshape(B * S, D)
    shared = None
    for layer in range(depth):
        if layer < n_a:
            i = layer
            h2 = _nsa_layer(h2, B, S, norm_mix[layer], nsa_w_in[i], nsa_q_gain[i], nsa_k_gain[i],
                            nsa_cmp_pos[i], nsa_cmp_w1[i], nsa_cmp_w2[i], nsa_w_out[i], sel_strip, win_strip)
        else:
            j = layer - n_a
            if shared is None:
                qt, k, vt, km = _shared_kv_and_q(h2, B, S, norm_mix[layer], kv_norm, kv_w, kv_k_gain,
                                                 moba_w_q[j], moba_q_gain[j])
                shared = (k, vt, km)
            else:
                qt = _moba_project(h2, norm_mix[layer], kv_norm, moba_w_q[j], kv_w, moba_q_gain[j],
                                   kv_k_gain, B, S)[0]
            k, vt, km = shared
            o = _moba_attn(qt, km, k, vt, sel_strip, B, S)
            h2 = _out_proj([o], moba_w_out[j], h2, B, S)
        h2 = _mlp(h2, norm_mlp[layer], mlp_w_up[layer], mlp_w_down[layer])
    return h2.reshape(B, S, D)
```

```python
import functools
import math

import jax
import jax.numpy as jnp
from jax import lax
from jax.experimental import pallas as pl
from jax.experimental.pallas import tpu as pltpu

F32 = jnp.float32
BF16 = jnp.bfloat16

N_HEADS = 16
HEAD_DIM = 64
N_GROUPS = 4
R = N_HEADS // N_GROUPS
LANES = 128
CMP_BLOCK = 32
CMP_STRIDE = 16
CMP_HIDDEN = 4 * HEAD_DIM
SEL_BLOCK = 64
SEL_TOPN = 16
N_FORCED = 3
WINDOW = 512
MOBA_BLOCK = 256
MOBA_TOPK = 3
REL_BUCKETS = 32
REL_MAX_DIST = 4096
RMS_EPS = 1e-6
NEG = -1e30
BIG = 1e9
LOG2E = math.log2(math.e)
GATE_ROWS = 16
ATTN_OUT_DTYPE = BF16
V_ROWS = HEAD_DIM + 16

TQ = 128
QSTEP = 512
NSUB = QSTEP // TQ
TK = 512
ROWS = R * TQ
QK_AHEAD = 4
FAR_DIST = 2897
DELTA_MAX = -(-(FAR_DIST + TK - 1) // LANES) * LANES
STRIP_W = -(-(DELTA_MAX + QSTEP) // 256) * 256
WIN_KEYS = WINDOW + TQ
WIN_STRIP_W = -(-(WINDOW + WIN_KEYS) // 256) * 256
VMEM_LIMIT = 56 * 1024 * 1024


def _cparams(n_axes):
    return pltpu.CompilerParams(dimension_semantics=("arbitrary",) * n_axes,
                                vmem_limit_bytes=VMEM_LIMIT)


def _dot(a, b):
    return jnp.dot(a, b, preferred_element_type=F32)


def _split3(x):
    parts = []
    rem = x
    for _ in range(3):
        hi = rem.astype(BF16)
        parts.append(hi)
        rem = rem - hi.astype(F32)
    return parts


def _strip_kernel(tab_ref, o_ref, *, c0, dlimit, ch):
    g = pl.program_id(0)
    j = pl.program_id(1)
    span = 2 * ch
    shape = (8, span)
    p = lax.broadcasted_iota(jnp.int32, shape, 1)
    d = jnp.where(p < ch, p, p - span) + c0 - j * ch
    n = jnp.maximum(d, 0)
    max_exact = REL_BUCKETS // 2
    nf = jnp.maximum(n, max_exact).astype(F32)
    large = max_exact + (jnp.log(nf / max_exact) / math.log(REL_MAX_DIST / max_exact)
                         * (REL_BUCKETS - max_exact)).astype(jnp.int32)
    large = jnp.minimum(large, REL_BUCKETS - 1)
    bucket = jnp.where(n < max_exact, n, large)
    ok = jnp.where(d >= 0, jnp.where(d < dlimit, 1, 0), 0)
    for r in range(R):
        h = g * R + r
        val = jnp.zeros(shape, F32)
        for b in range(REL_BUCKETS):
            val = jnp.where(bucket == b, tab_ref[b, h], val)
        val = val - tab_ref[REL_BUCKETS - 1, h]
        w = jnp.where(ok > 0, val * LOG2E, NEG)
        wb = jnp.broadcast_to(w[0:1, :], (ch, span))
        o_ref[:, r * TQ:(r + 1) * TQ] = pltpu.roll(wb, 0, 1, stride=1, stride_axis=0)[:, :TQ].astype(o_ref.dtype)


def _bias_strip(rel_table, width, c0, dlimit):
    ch = 256
    return pl.pallas_call(
        functools.partial(_strip_kernel, c0=c0, dlimit=dlimit, ch=ch),
        grid=(N_GROUPS, width // ch),
        in_specs=[pl.BlockSpec(memory_space=pltpu.SMEM)],
        out_specs=pl.BlockSpec((None, ch, ROWS), lambda g, j: (g, j, 0)),
        out_shape=jax.ShapeDtypeStruct((N_GROUPS, width, ROWS), BF16),
        compiler_params=_cparams(2),
    )(rel_table.astype(F32))


def _head_norm(y, bd_ref, gain):
    outs = []
    for c in range(y.shape[1] // LANES):
        yc = y[:, c * LANES:(c + 1) * LANES]
        y2 = yc * yc
        hi = y2.astype(BF16)
        lo = (y2 - hi.astype(F32)).astype(BF16)
        ss = _dot(hi, bd_ref[...]) + _dot(lo, bd_ref[...])
        outs.append(yc * lax.rsqrt(ss * (1.0 / HEAD_DIM) + RMS_EPS))
    return jnp.concatenate(outs, axis=1) * gain


def _store_heads_t(q_ref, y):
    yt = y.T
    row = lax.broadcasted_iota(jnp.int32, (LANES, y.shape[0]), 0)
    for h in range(N_HEADS):
        pair = yt[(h // 2) * LANES:(h // 2 + 1) * LANES, :]
        keep = row < HEAD_DIM if h % 2 == 0 else row >= HEAD_DIM
        q_ref[h] = jnp.where(keep, pair, 0.0).astype(q_ref.dtype)


def _store_groups_dup(o_ref, y):
    lane = lax.broadcasted_iota(jnp.int32, (y.shape[0], LANES), 1)
    for g in range(N_GROUPS):
        pair = y[:, (g // 2) * LANES:(g // 2 + 1) * LANES]
        keep = lane < HEAD_DIM if g % 2 == 0 else lane >= HEAD_DIM
        m = jnp.where(keep, pair, 0.0)
        o_ref[g] = (m + pltpu.roll(m, HEAD_DIM, 1)).astype(o_ref.dtype)


def _store_groups_t(o_ref, y):
    yt = y.T
    ones = jnp.ones((V_ROWS - HEAD_DIM, y.shape[0]), o_ref.dtype)
    for g in range(N_GROUPS):
        o_ref[g, 0:HEAD_DIM, :] = yt[g * HEAD_DIM:(g + 1) * HEAD_DIM, :].astype(o_ref.dtype)
        o_ref[g, HEAD_DIM:V_ROWS, :] = ones


def _nsa_proj_kernel(x_ref, gn_ref, w_ref, bd_ref, qg_ref, kg_ref,
                     q_ref, cmp_ref, ksel_ref, vsel_ref, kwin_ref, vwin_ref, gate_ref):
    x = x_ref[...]
    ms = jnp.mean(x * x, axis=-1, keepdims=True)
    xn = (x * lax.rsqrt(ms + RMS_EPS) * gn_ref[...]).astype(BF16)
    d = N_HEADS * HEAD_DIM
    gd = N_GROUPS * HEAD_DIM
    yq = _dot(xn, w_ref[:, 0:d])
    _store_heads_t(q_ref, _head_norm(yq, bd_ref, qg_ref[...]))
    ycmp = _dot(xn, w_ref[:, d:d + 2 * gd])
    for c in range(2 * gd // LANES):
        cmp_ref[c] = ycmp[:, c * LANES:(c + 1) * LANES]
    ysel = _dot(xn, w_ref[:, d + 2 * gd:d + 4 * gd])
    _store_groups_dup(ksel_ref, _head_norm(ysel[:, :gd], bd_ref, kg_ref[0:1, :]))
    _store_groups_t(vsel_ref, ysel[:, gd:])
    ywin = _dot(xn, w_ref[:, d + 4 * gd:d + 6 * gd])
    _store_groups_dup(kwin_ref, _head_norm(ywin[:, :gd], bd_ref, kg_ref[1:2, :]))
    _store_groups_t(vwin_ref, ywin[:, gd:])
    yg = _dot(xn, w_ref[:, d + 6 * gd:])
    sgt = (1.0 / (1.0 + jnp.exp(-yg))).T
    for g in range(N_GROUPS):
        gate_ref[g] = sgt[g * LANES:g * LANES + GATE_ROWS, :]


def _qk_side_inputs(q_gain):
    bd = jnp.kron(jnp.eye(2, dtype=F32), jnp.ones((HEAD_DIM, HEAD_DIM), F32)).astype(BF16)
    qg = jnp.tile(q_gain.astype(F32) * (HEAD_DIM ** -0.5 * LOG2E), N_HEADS)[None, :]
    return bd, qg


def _nsa_project(x2, gain, w_in, q_gain, k_gain, B, S):
    D = x2.shape[1]
    d = N_HEADS * HEAD_DIM
    gd = N_GROUPS * HEAD_DIM
    wg = w_in[:, d + 6 * gd:].reshape(D, N_GROUPS, 3 * R)
    wg = jnp.pad(wg, ((0, 0), (0, 0), (0, LANES - 3 * R))).reshape(D, N_GROUPS * LANES)
    w = jnp.concatenate([w_in[:, :d + 6 * gd], wg], axis=1).astype(BF16)
    ncol = w.shape[1]
    bd, qg = _qk_side_inputs(q_gain)
    kg = jnp.stack([jnp.tile(k_gain[1].astype(F32), N_GROUPS), jnp.tile(k_gain[2].astype(F32), N_GROUPS)])
    tm = 256
    nt = S // tm
    kspec = pl.BlockSpec((None, N_GROUPS, tm, LANES), lambda i: (i // nt, 0, i % nt, 0))
    vspec = pl.BlockSpec((None, N_GROUPS, V_ROWS, tm), lambda i: (i // nt, 0, 0, i % nt))
    kshape = jax.ShapeDtypeStruct((B, N_GROUPS, S, LANES), BF16)
    vshape = jax.ShapeDtypeStruct((B, N_GROUPS, V_ROWS, S), BF16)
    return pl.pallas_call(
        _nsa_proj_kernel,
        grid=(B * nt,),
        in_specs=[
            pl.BlockSpec((tm, D), lambda i: (i, 0)),
            pl.BlockSpec((1, D), lambda i: (0, 0)),
            pl.BlockSpec((D, ncol), lambda i: (0, 0)),
            pl.BlockSpec((LANES, LANES), lambda i: (0, 0)),
            pl.BlockSpec((1, d), lambda i: (0, 0)),
            pl.BlockSpec((2, gd), lambda i: (0, 0)),
        ],
        out_specs=[
            pl.BlockSpec((None, N_HEADS, LANES, tm), lambda i: (i // nt, 0, 0, i % nt)),
            pl.BlockSpec((2 * gd // LANES, tm, LANES), lambda i: (0, i, 0)),
            kspec, vspec, kspec, vspec,
            pl.BlockSpec((None, N_GROUPS, GATE_ROWS, tm), lambda i: (i // nt, 0, 0, i % nt)),
        ],
        out_shape=[
            jax.ShapeDtypeStruct((B, N_HEADS, LANES, S), BF16),
            jax.ShapeDtypeStruct((2 * gd // LANES, B * S, LANES), F32),
            kshape, vshape, kshape, vshape,
            jax.ShapeDtypeStruct((B, N_GROUPS, GATE_ROWS, S), F32),
        ],
        compiler_params=_cparams(1),
    )(x2, gain[None, :].astype(F32), w, bd, qg, kg)


def _moba_proj_kernel(x_ref, gq_ref, gkv_ref, wq_ref, wkv_ref, bd_ref, qg_ref, kg_ref,
                      q_ref, k_ref, v_ref, kmean_ref):
    x = x_ref[...]
    ms = jnp.mean(x * x, axis=-1, keepdims=True)
    xr = x * lax.rsqrt(ms + RMS_EPS)
    gd = N_GROUPS * HEAD_DIM
    yq = _dot((xr * gq_ref[...]).astype(BF16), wq_ref[...])
    _store_heads_t(q_ref, _head_norm(yq, bd_ref, qg_ref[...]))
    ykv = _dot((xr * gkv_ref[...]).astype(BF16), wkv_ref[...])
    kn = _head_norm(ykv[:, :gd], bd_ref, kg_ref[...])
    _store_groups_dup(k_ref, kn)
    _store_groups_t(v_ref, ykv[:, gd:])
    kmean_ref[...] = jnp.mean(kn, axis=0, keepdims=True)


def _moba_project(h2, g_mix, g_kv, w_q, kv_w, q_gain, k_gain, B, S):
    D = h2.shape[1]
    d = N_HEADS * HEAD_DIM
    gd = N_GROUPS * HEAD_DIM
    bd, qg = _qk_side_inputs(q_gain)
    kg = jnp.tile(k_gain.astype(F32), N_GROUPS)[None, :]
    tm = MOBA_BLOCK
    nt = S // tm
    return pl.pallas_call(
        _moba_proj_kernel,
        grid=(B * nt,),
        in_specs=[
            pl.BlockSpec((tm, D), lambda i: (i, 0)),
            pl.BlockSpec((1, D), lambda i: (0, 0)),
            pl.BlockSpec((1, D), lambda i: (0, 0)),
            pl.BlockSpec((D, d), lambda i: (0, 0)),
            pl.BlockSpec((D, 2 * gd), lambda i: (0, 0)),
            pl.BlockSpec((LANES, LANES), lambda i: (0, 0)),
            pl.BlockSpec((1, d), lambda i: (0, 0)),
            pl.BlockSpec((1, gd), lambda i: (0, 0)),
        ],
        out_specs=[
            pl.BlockSpec((None, N_HEADS, LANES, tm), lambda i: (i // nt, 0, 0, i % nt)),
            pl.BlockSpec((None, N_GROUPS, tm, LANES), lambda i: (i // nt, 0, i % nt, 0)),
            pl.BlockSpec((None, N_GROUPS, V_ROWS, tm), lambda i: (i // nt, 0, 0, i % nt)),
            pl.BlockSpec((None, 1, gd), lambda i: (i, 0, 0)),
        ],
        out_shape=[
            jax.ShapeDtypeStruct((B, N_HEADS, LANES, S), BF16),
            jax.ShapeDtypeStruct((B, N_GROUPS, S, LANES), BF16),
            jax.ShapeDtypeStruct((B, N_GROUPS, V_ROWS, S), BF16),
            jax.ShapeDtypeStruct((B * nt, 1, gd), F32),
        ],
        compiler_params=_cparams(1),
    )(h2, g_mix[None, :].astype(F32), g_kv[None, :].astype(F32), w_q.astype(BF16), kv_w.astype(BF16),
      bd, qg, kg)


def _compress_kernel(x_ref, pos_ref, w1_ref, w2_ref, kg_ref, kc_ref, vct_ref, *, n_chunk):
    t = pl.program_id(1)
    half = CMP_BLOCK // 2
    za = [[], []]
    zb = [[], []]
    for l in range(half):
        for p in range(2):
            xl = x_ref[p, pl.ds(l, n_chunk, stride=CMP_STRIDE), :]
            za[p].append((xl + pos_ref[l:l + 1, p * LANES:(p + 1) * LANES]).astype(BF16))
            zb[p].append((xl + pos_ref[half + l:half + l + 1, p * LANES:(p + 1) * LANES]).astype(BF16))
    za = [jnp.concatenate(z, axis=1) for z in za]
    zb = [jnp.concatenate(z, axis=1) for z in zb]
    for g in range(N_GROUPS):
        p, e = g // 2, g % 2
        first = _dot(za[p], w1_ref[0, e])
        second = _dot(zb[p], w1_ref[1, e])
        hid = first + pltpu.roll(second, n_chunk - 1, 0)
        hid = hid * (1.0 / (1.0 + jnp.exp(-hid)))
        out = _dot(hid.astype(BF16), w2_ref[...])

        @pl.when(t == 0)
        def _():
            normed = out * lax.rsqrt(jnp.mean(out * out, axis=-1, keepdims=True) + RMS_EPS) * kg_ref[...]
            kc_ref[g] = normed.astype(kc_ref.dtype)

        @pl.when(t == 1)
        def _():
            vct_ref[g] = out.T[:HEAD_DIM, :].astype(vct_ref.dtype)


def _nsa_compress(cmp_raw, cmp_pos, cmp_w1, cmp_w2, k_gain0, B, S):
    gd = N_GROUPS * HEAD_DIM
    half = CMP_BLOCK // 2
    n_chunk = S // CMP_STRIDE
    pos = jnp.tile(cmp_pos.astype(F32), (1, 1, N_GROUPS))
    w1 = cmp_w1.astype(F32).reshape(2, 2, half, HEAD_DIM, CMP_HIDDEN)
    z = jnp.zeros_like(w1)
    w1p = jnp.stack([jnp.concatenate([w1, z], axis=3), jnp.concatenate([z, w1], axis=3)], axis=2)
    w1p = w1p.reshape(2, 2, 2, half * LANES, CMP_HIDDEN).astype(BF16)
    w2 = jnp.concatenate([cmp_w2, cmp_w2], axis=-1).astype(BF16)
    kg = jnp.tile(k_gain0.astype(F32), 2)[None, :]
    return pl.pallas_call(
        functools.partial(_compress_kernel, n_chunk=n_chunk),
        grid=(B, 2),
        in_specs=[
            pl.BlockSpec((2, S, LANES), lambda b, t: (t, b, 0)),
            pl.BlockSpec((None, CMP_BLOCK, gd), lambda b, t: (t, 0, 0)),
            pl.BlockSpec((None, 2, 2, half * LANES, CMP_HIDDEN), lambda b, t: (t, 0, 0, 0, 0)),
            pl.BlockSpec((None, CMP_HIDDEN, LANES), lambda b, t: (t, 0, 0)),
            pl.BlockSpec((1, LANES), lambda b, t: (0, 0)),
        ],
        out_specs=[
            pl.BlockSpec((None, N_GROUPS, n_chunk, LANES), lambda b, t: (b, 0, 0, 0)),
            pl.BlockSpec((None, N_GROUPS, HEAD_DIM, n_chunk), lambda b, t: (b, 0, 0, 0)),
        ],
        out_shape=[
            jax.ShapeDtypeStruct((B, N_GROUPS, n_chunk, LANES), BF16),
            jax.ShapeDtypeStruct((B, N_GROUPS, HEAD_DIM, n_chunk), BF16),
        ],
        compiler_params=_cparams(2),
    )(cmp_raw, pos, w1p, w2, kg)


_Q_SPEC = pl.BlockSpec((None, R, LANES, QSTEP), lambda b, g, i: (b, g, 0, i))
_GATE_SPEC = pl.BlockSpec((None, None, GATE_ROWS, QSTEP), lambda b, g, i: (b, g, 0, i))
_O_SPEC = pl.BlockSpec((None, R * HEAD_DIM, QSTEP), lambda b, g, i: (b, g, i))


def _chains():
    return [(u, hf) for u in range(NSUB) for hf in range(R // 2)]


def _run_chains(scores, finish, ahead=None, next_scores=None):
    chains = _chains()
    ahead = [scores(ch) for ch in chains[:QK_AHEAD]] if ahead is None else list(ahead)
    for idx, ch in enumerate(chains):
        s = ahead.pop(0)
        nxt = idx + QK_AHEAD
        if nxt < len(chains):
            ahead.append(scores(chains[nxt]))
        elif next_scores is not None:
            ahead.append(next_scores(chains[nxt - len(chains)]))
        finish(ch, s)
    return ahead


def _chain_qt(q_ref, u, hf):
    return jnp.concatenate([q_ref[2 * hf + e, :, u * TQ:(u + 1) * TQ] for e in range(2)], axis=1)


def _store_chain_out(o_ref, ot, gate_ref, gate_col, u, hf):
    for e in range(2):
        r = 2 * hf + e
        o_r = ot[:, e * TQ:(e + 1) * TQ]
        if gate_ref is not None:
            c = 3 * r + gate_col
            o_r = o_r * gate_ref[c:c + 1, u * TQ:(u + 1) * TQ]
        o_ref[r * HEAD_DIM:(r + 1) * HEAD_DIM, u * TQ:(u + 1) * TQ] = o_r.astype(o_ref.dtype)


def _topk_mark_t(score, k):
    row = lax.broadcasted_iota(jnp.int32, score.shape, 0).astype(F32)
    s = score
    for _ in range(k):
        m = jnp.max(s, axis=0, keepdims=True)
        idx = jnp.min(jnp.where(s == m, row, float(LANES)), axis=0, keepdims=True)
        s = jnp.where(row == idx, -jnp.inf, s)
    return s


_FLASH_SCRATCH = [pltpu.VMEM((QK_AHEAD, TK, 2 * TQ), BF16), pltpu.VMEM((QK_AHEAD, 8, 2 * TQ), F32)]


def _flash_tiles(qas, k_ref, oh_ref, vt_ref, strip_ref, s_scr, mx_scr, q0, n_tiles):
    chains = _chains()
    cw = 2 * TQ
    last_k0 = k_ref.shape[0] - TK

    def tile_scores(kt):
        k0 = pl.multiple_of(jnp.minimum(kt * TK, last_k0), TK)
        ka = jnp.concatenate([oh_ref[pl.ds(k0, TK), :], k_ref[pl.ds(k0, TK), :]], axis=1)

        def scores(ch):
            u, hf = ch
            cs = pl.multiple_of(DELTA_MAX - jnp.clip(q0 + u * TQ - k0, 0, DELTA_MAX), LANES)
            s = _dot(ka, qas[ch]).astype(BF16) + strip_ref[pl.ds(cs, TK), hf * cw:(hf + 1) * cw]
            return s, jnp.max(s, axis=0, keepdims=True).astype(F32)

        return scores

    def park(slot, s_and_max):
        s_scr[slot] = s_and_max[0]
        mx_scr[slot] = jnp.broadcast_to(s_and_max[1], mx_scr.shape[1:])

    def body(kt, state):
        vt = vt_ref[:, pl.ds(pl.multiple_of(kt * TK, TK), TK)]
        cur, nxt = tile_scores(kt), tile_scores(kt + 1)
        in_flight = {}
        out = []
        for idx, ch in enumerate(chains):
            s, s_max = (s_scr[idx], mx_scr[idx, 0:1]) if idx < QK_AHEAD else in_flight.pop(idx)
            ahead = idx + QK_AHEAD
            if ahead < len(chains):
                in_flight[ahead] = cur(chains[ahead])
            else:
                park(ahead - len(chains), nxt(chains[ahead - len(chains)]))
            m, acc = state[idx]
            m_new = jnp.maximum(m, s_max)
            p = jnp.exp2(s - m_new.astype(BF16))
            out.append((m_new, jnp.exp2(m - m_new) * acc + _dot(vt, p)))
        return tuple(out)

    first = tile_scores(0)
    for slot in range(QK_AHEAD):
        park(slot, first(chains[slot]))
    init = tuple((jnp.full((1, cw), 3 * NEG, F32), jnp.zeros((V_ROWS, cw), F32)) for _ in chains)
    fin = lax.fori_loop(0, n_tiles, body, init)
    return {ch: acc[:HEAD_DIM] * (1.0 / acc[HEAD_DIM:HEAD_DIM + 1]) for ch, (m, acc) in zip(chains, fin)}


def _nsa_cmp_kernel(q_ref, kc_ref, vct_ref, vis_ref, c2st_ref, gate_ref, o_ref, mb_ref, imp_scr, *, n_chunk):
    q0 = pl.program_id(2) * QSTEP
    cw = 2 * TQ
    i_col = lax.broadcasted_iota(jnp.int32, (1, cw), 1) & (TQ - 1)

    def attend(n_used):
        psums = [None] * NSUB

        def scores(ch):
            r0 = pl.multiple_of(n_chunk - (q0 + ch[0] * TQ) // CMP_STRIDE, 8)
            return _dot(kc_ref[0:n_used, :], _chain_qt(q_ref, *ch)) + vis_ref[pl.ds(r0, n_used), :]

        def finish(ch, lcm):
            u, hf = ch
            e = jnp.exp2(lcm - jnp.max(lcm, axis=0, keepdims=True))
            any_visible = q0 + u * TQ + i_col >= CMP_BLOCK - 1
            pc = e * jnp.where(any_visible, 1.0 / jnp.sum(e, axis=0, keepdims=True), 0.0)
            oc = _dot(vct_ref[:, 0:n_used], pc.astype(BF16))
            _store_chain_out(o_ref, oc, gate_ref, 0, u, hf)
            ps = pc[:, :TQ] + pc[:, TQ:]
            psums[u] = ps if psums[u] is None else psums[u] + ps

        _run_chains(scores, finish)
        psum = jnp.concatenate(psums, axis=1)
        imp = None
        for part in _split3(psum):
            term = _dot(c2st_ref[:, 0:n_used], part)
            imp = term if imp is None else imp + term
        imp_scr[...] = imp

    span = LANES * CMP_STRIDE
    for v in range(1, n_chunk // LANES + 1):
        pl.when(q0 // span + 1 == v)(functools.partial(attend, LANES * v))

    imp = imp_scr[...]
    cur = (q0 + lax.broadcasted_iota(jnp.int32, (LANES, QSTEP), 1)) // SEL_BLOCK
    j = lax.broadcasted_iota(jnp.int32, (LANES, QSTEP), 0)
    forced_or_imp = jnp.where(j == 0, -jnp.inf, jnp.where(j >= cur - 1, -jnp.inf, imp))
    score = jnp.where(j <= cur, forced_or_imp, -BIG)
    chosen = _topk_mark_t(score, SEL_TOPN - N_FORCED)
    mb_ref[...] = jnp.where(chosen == -jnp.inf, 0.0, NEG).astype(mb_ref.dtype)


def _nsa_cmp(qt, kc, vct, gates, B, S):
    n_chunk = S // CMP_STRIDE
    n_sel = S // SEL_BLOCK
    assert n_sel <= LANES and n_chunk % LANES == 0
    cs = jnp.arange(n_chunk)[None, :] * CMP_STRIDE
    ss = jnp.arange(LANES)[:, None] * SEL_BLOCK
    overlap = jnp.clip(jnp.minimum(cs + CMP_BLOCK, ss + SEL_BLOCK) - jnp.maximum(cs, ss), 0, None) / CMP_BLOCK
    n_cmp = (S - CMP_BLOCK) // CMP_STRIDE + 1
    c2st = jnp.where((jnp.arange(n_chunk)[None, :] < n_cmp) & (jnp.arange(LANES)[:, None] < n_sel),
                     overlap, 0.0).astype(BF16)
    c_end = (jnp.arange(2 * n_chunk)[:, None] - n_chunk) * CMP_STRIDE + (CMP_BLOCK - 1)
    vis = jnp.where(c_end <= (jnp.arange(2 * TQ)[None, :] & (TQ - 1)), 0.0, NEG).astype(F32)
    return pl.pallas_call(
        functools.partial(_nsa_cmp_kernel, n_chunk=n_chunk),
        grid=(B, N_GROUPS, S // QSTEP),
        in_specs=[
            _Q_SPEC,
            pl.BlockSpec((None, None, n_chunk, LANES), lambda b, g, i: (b, g, 0, 0)),
            pl.BlockSpec((None, None, HEAD_DIM, n_chunk), lambda b, g, i: (b, g, 0, 0)),
            pl.BlockSpec((2 * n_chunk, 2 * TQ), lambda b, g, i: (0, 0)),
            pl.BlockSpec((LANES, n_chunk), lambda b, g, i: (0, 0)),
            _GATE_SPEC,
        ],
        out_specs=[
            _O_SPEC,
            pl.BlockSpec((None, None, LANES, QSTEP), lambda b, g, i: (b, g, 0, i)),
        ],
        out_shape=[
            jax.ShapeDtypeStruct((B, N_HEADS * HEAD_DIM, S), ATTN_OUT_DTYPE),
            jax.ShapeDtypeStruct((B, N_GROUPS, LANES, S), BF16),
        ],
        scratch_shapes=[pltpu.VMEM((LANES, QSTEP), F32)],
        compiler_params=_cparams(3),
    )(qt, kc, vct, vis, c2st, gates)


def _nsa_sel_kernel(q_ref, mb_ref, k_ref, oh_ref, vt_ref, strip_ref, gate_ref, o_ref, s_scr, mx_scr):
    q0 = pl.program_id(2) * QSTEP
    qas = {}
    for u, hf in _chains():
        mb = mb_ref[:, u * TQ:(u + 1) * TQ]
        qas[(u, hf)] = jnp.concatenate([jnp.concatenate([mb, mb], axis=1), _chain_qt(q_ref, u, hf)], axis=0)
    n_tiles = (q0 + QSTEP) // TK
    o = _flash_tiles(qas, k_ref, oh_ref, vt_ref, strip_ref, s_scr, mx_scr, q0, n_tiles)
    for u, hf in _chains():
        _store_chain_out(o_ref, o[(u, hf)], gate_ref, 1, u, hf)


def _nsa_sel(qt, mb, ksel, vsel_t, strip, gates, B, S):
    onehot = (jnp.arange(S)[:, None] // SEL_BLOCK == jnp.arange(LANES)[None, :]).astype(BF16)
    return pl.pallas_call(
        _nsa_sel_kernel,
        grid=(B, N_GROUPS, S // QSTEP),
        in_specs=[
            _Q_SPEC,
            pl.BlockSpec((None, None, LANES, QSTEP), lambda b, g, i: (b, g, 0, i)),
            pl.BlockSpec((None, None, S, LANES), lambda b, g, i: (b, g, 0, 0)),
            pl.BlockSpec((S, LANES), lambda b, g, i: (0, 0)),
            pl.BlockSpec((None, None, V_ROWS, S), lambda b, g, i: (b, g, 0, 0)),
            pl.BlockSpec((None, STRIP_W, ROWS), lambda b, g, i: (g, 0, 0)),
            _GATE_SPEC,
        ],
        out_specs=_O_SPEC,
        out_shape=jax.ShapeDtypeStruct((B, N_HEADS * HEAD_DIM, S), ATTN_OUT_DTYPE),
        scratch_shapes=_FLASH_SCRATCH,
        compiler_params=_cparams(3),
    )(qt, mb, ksel, onehot, vsel_t, strip, gates)


def _nsa_win_kernel(q_ref, k_ref, vt_ref, strip_ref, gate_ref, o_ref):
    q0 = pl.program_id(2) * QSTEP
    cw = 2 * TQ

    def key_start(u):
        return pl.multiple_of(jnp.maximum(q0 + u * TQ - WINDOW, 0), TQ)

    def scores(ch):
        u, hf = ch
        k0 = key_start(u)
        cs = pl.multiple_of(WINDOW - (q0 + u * TQ - k0), LANES)
        return (_dot(k_ref[pl.ds(k0, WIN_KEYS), :], _chain_qt(q_ref, u, hf)).astype(BF16)
                + strip_ref[pl.ds(cs, WIN_KEYS), hf * cw:(hf + 1) * cw])

    def finish(ch, s):
        u, hf = ch
        e = jnp.exp2(s - jnp.max(s, axis=0, keepdims=True))
        o = _dot(vt_ref[:, pl.ds(key_start(u), WIN_KEYS)], e)
        o = o[:HEAD_DIM] * (1.0 / o[HEAD_DIM:HEAD_DIM + 1])
        _store_chain_out(o_ref, o, gate_ref, 2, u, hf)

    _run_chains(scores, finish)


def _nsa_win(qt, kwin, vwin_t, strip, gates, B, S):
    assert S >= WIN_KEYS
    return pl.pallas_call(
        _nsa_win_kernel,
        grid=(B, N_GROUPS, S // QSTEP),
        in_specs=[
            _Q_SPEC,
            pl.BlockSpec((None, None, S, LANES), lambda b, g, i: (b, g, 0, 0)),
            pl.BlockSpec((None, None, V_ROWS, S), lambda b, g, i: (b, g, 0, 0)),
            pl.BlockSpec((None, WIN_STRIP_W, ROWS), lambda b, g, i: (g, 0, 0)),
            _GATE_SPEC,
        ],
        out_specs=_O_SPEC,
        out_shape=jax.ShapeDtypeStruct((B, N_HEADS * HEAD_DIM, S), ATTN_OUT_DTYPE),
        compiler_params=_cparams(3),
    )(qt, kwin, vwin_t, strip, gates)


def _moba_kernel(q_ref, kmean_ref, k_ref, oh_ref, vt_ref, strip_ref, o_ref, s_scr, mx_scr):
    q0 = pl.program_id(2) * QSTEP
    cw = 2 * TQ
    nb = kmean_ref.shape[0]
    n = lax.broadcasted_iota(jnp.int32, (nb, cw), 0)
    no_block = jnp.full((LANES - nb, cw), NEG, BF16)
    qas = {}
    for u, hf in _chains():
        cblk = (q0 + u * TQ) // MOBA_BLOCK
        qt = _chain_qt(q_ref, u, hf)
        gate = _dot(kmean_ref[...], qt)
        past = n < cblk
        chosen = _topk_mark_t(jnp.where(past, gate, NEG), MOBA_TOPK)
        mb = jnp.where(n == cblk, 0.0, jnp.where(past, jnp.where(chosen == -jnp.inf, 0.0, NEG), NEG)).astype(BF16)
        qas[(u, hf)] = jnp.concatenate([mb, no_block, qt], axis=0)
    n_tiles = (q0 + QSTEP) // TK
    o = _flash_tiles(qas, k_ref, oh_ref, vt_ref, strip_ref, s_scr, mx_scr, q0, n_tiles)
    for u, hf in _chains():
        _store_chain_out(o_ref, o[(u, hf)], None, 0, u, hf)


def _moba_attn(qt, kmean, k, vt, strip, B, S):
    n_blk = S // MOBA_BLOCK
    assert n_blk <= LANES
    onehot = (jnp.arange(S)[:, None] // MOBA_BLOCK == jnp.arange(LANES)[None, :]).astype(BF16)
    return pl.pallas_call(
        _moba_kernel,
        grid=(B, N_GROUPS, S // QSTEP),
        in_specs=[
            _Q_SPEC,
            pl.BlockSpec((None, None, kmean.shape[2], LANES), lambda b, g, i: (b, g, 0, 0)),
            pl.BlockSpec((None, None, S, LANES), lambda b, g, i: (b, g, 0, 0)),
            pl.BlockSpec((S, LANES), lambda b, g, i: (0, 0)),
            pl.BlockSpec((None, None, V_ROWS, S), lambda b, g, i: (b, g, 0, 0)),
            pl.BlockSpec((None, STRIP_W, ROWS), lambda b, g, i: (g, 0, 0)),
        ],
        out_specs=_O_SPEC,
        out_shape=jax.ShapeDtypeStruct((B, N_HEADS * HEAD_DIM, S), ATTN_OUT_DTYPE),
        scratch_shapes=_FLASH_SCRATCH,
        compiler_params=_cparams(3),
    )(qt, kmean, k, onehot, vt, strip)


def _out_proj_kernel(*refs):
    *o_refs, w_ref, h_ref, out_ref = refs
    o = o_refs[0][...]
    if len(o_refs) > 1:
        o = o.astype(F32)
        for r in o_refs[1:]:
            o = o + r[...].astype(F32)
    out_ref[...] = h_ref[...] + lax.dot_general(o.astype(BF16), w_ref[...], (((0,), (0,)), ((), ())),
                                                 preferred_element_type=F32)


def _out_proj(parts_t, w_out, h2, B, S):
    d = parts_t[0].shape[1]
    D = w_out.shape[1]
    tm = 512
    nt = S // tm
    return pl.pallas_call(
        _out_proj_kernel,
        grid=(B * nt,),
        in_specs=[pl.BlockSpec((None, d, tm), lambda i: (i // nt, 0, i % nt)) for _ in parts_t] + [
            pl.BlockSpec((d, D), lambda i: (0, 0)),
            pl.BlockSpec((tm, D), lambda i: (i, 0)),
        ],
        out_specs=pl.BlockSpec((tm, D), lambda i: (i, 0)),
        out_shape=jax.ShapeDtypeStruct((B * S, D), F32),
        compiler_params=_cparams(1),
    )(*parts_t, w_out.astype(BF16), h2)


def _mlp_kernel(h_ref, g_ref, wu_ref, wd_ref, out_ref, *, ff_chunk):
    h = h_ref[...]
    ms = jnp.mean(h * h, axis=-1, keepdims=True)
    xn = (h * lax.rsqrt(ms + RMS_EPS) * g_ref[...]).astype(BF16)
    acc = h
    for c in range(wu_ref.shape[1] // ff_chunk):
        a = jnp.maximum(_dot(xn, wu_ref[:, c * ff_chunk:(c + 1) * ff_chunk]), 0.0)
        acc = acc + _dot((a * a).astype(BF16), wd_ref[c * ff_chunk:(c + 1) * ff_chunk, :])
    out_ref[...] = acc


def _mlp(h2, gain, w_up, w_down):
    n, D = h2.shape
    F = w_up.shape[1]
    tm = 256
    return pl.pallas_call(
        functools.partial(_mlp_kernel, ff_chunk=1024),
        grid=(n // tm,),
        in_specs=[
            pl.BlockSpec((tm, D), lambda i: (i, 0)),
            pl.BlockSpec((1, D), lambda i: (0, 0)),
            pl.BlockSpec((D, F), lambda i: (0, 0)),
            pl.BlockSpec((F, D), lambda i: (0, 0)),
        ],
        out_specs=pl.BlockSpec((tm, D), lambda i: (i, 0)),
        out_shape=jax.ShapeDtypeStruct((n, D), F32),
        compiler_params=_cparams(1),
    )(h2, gain[None, :].astype(F32), w_up.astype(BF16), w_down.astype(BF16))


def _nsa_layer(h2, B, S, norm_mix, w_in, q_gain, k_gain, cmp_pos, cmp_w1, cmp_w2, w_out, sel_strip, win_strip):
    qt, cmp_raw, ksel, vsel_t, kwin, vwin_t, gates = _nsa_project(h2, norm_mix, w_in, q_gain, k_gain, B, S)
    kc, vct = _nsa_compress(cmp_raw, cmp_pos, cmp_w1, cmp_w2, k_gain[0], B, S)
    o_cmp, mb = _nsa_cmp(qt, kc, vct, gates, B, S)
    o_sel = _nsa_sel(qt, mb, ksel, vsel_t, sel_strip, gates, B, S)
    o_win = _nsa_win(qt, kwin, vwin_t, win_strip, gates, B, S)
    return _out_proj([o_cmp, o_sel, o_win], w_out, h2, B, S)


def _shared_kv_and_q(h2, B, S, norm_mix, kv_norm, kv_w, kv_k_gain, w_q, q_gain):
    qt, k, vt, kmean = _moba_project(h2, norm_mix, kv_norm, w_q, kv_w, q_gain, kv_k_gain, B, S)
    n_blk = S // MOBA_BLOCK
    km = kmean.reshape(B, n_blk, N_GROUPS, HEAD_DIM).transpose(0, 2, 1, 3)
    km = jnp.concatenate([km, km], axis=-1)
    km = jnp.pad(km, ((0, 0), (0, 0), (0, -n_blk % 16), (0, 0))).astype(BF16)
    return qt, k, vt, km


def kernel(x, norm_mix, norm_mlp, nsa_w_in, nsa_q_gain, nsa_k_gain, nsa_cmp_pos, nsa_cmp_w1, nsa_cmp_w2,
           nsa_w_out, kv_norm, kv_w, kv_k_gain, moba_w_q, moba_q_gain, moba_w_out, rel_table, mlp_w_up,
           mlp_w_down):
    B, S, D = x.shape
    depth = norm_mix.shape[0]
    n_a = nsa_w_in.shape[0]
    assert S % TK == 0 and S % MOBA_BLOCK == 0
    sel_strip = _bias_strip(rel_table, STRIP_W, DELTA_MAX, 1 << 30)
    win_strip = _bias_strip(rel_table, WIN_STRIP_W, WINDOW, WINDOW)
    h2 = x.reshape(B * S, D)
    shared = None
    for layer in range(depth):
        if layer < n_a:
            i = layer
            h2 = _nsa_layer(h2, B, S, norm_mix[layer], nsa_w_in[i], nsa_q_gain[i], nsa_k_gain[i],
                            nsa_cmp_pos[i], nsa_cmp_w1[i], nsa_cmp_w2[i], nsa_w_out[i], sel_strip, win_strip)
        else:
            j = layer - n_a
            if shared is None:
                qt, k, vt, km = _shared_kv_and_q(h2, B, S, norm_mix[layer], kv_norm, kv_w, kv_k_gain,
                                                 moba_w_q[j], moba_q_gain[j])
                shared = (k, vt, km)
            else:
                qt = _moba_project(h2, norm_mix[layer], kv_norm, moba_w_q[j], kv_w, moba_q_gain[j],
                                   kv_k_gain, B, S)[0]
            k, vt, km = shared
            o = _moba_attn(qt, km, k, vt, sel_strip, B, S)
            h2 = _out_proj([o], moba_w_out[j], h2, B, S)
        h2 = _mlp(h2, norm_mlp[layer], mlp_w_up[layer], mlp_w_down[layer])
    return h2.reshape(B, S, D)
```

```python
import functools
import math

import jax
import jax.numpy as jnp
from jax import lax
from jax.experimental import pallas as pl
from jax.experimental.pallas import tpu as pltpu

F32 = jnp.float32
BF16 = jnp.bfloat16

N_HEADS = 16
HEAD_DIM = 64
N_GROUPS = 4
R = N_HEADS // N_GROUPS
LANES = 128
CMP_BLOCK = 32
CMP_STRIDE = 16
CMP_HIDDEN = 4 * HEAD_DIM
SEL_BLOCK = 64
SEL_TOPN = 16
N_FORCED = 3
WINDOW = 512
MOBA_BLOCK = 256
MOBA_TOPK = 3
REL_BUCKETS = 32
REL_MAX_DIST = 4096
RMS_EPS = 1e-6
NEG = -1e30
BIG = 1e9
LOG2E = math.log2(math.e)
GATE_ROWS = 16
ATTN_OUT_DTYPE = BF16
V_ROWS = HEAD_DIM + 16

TQ = 128
QSTEP = 512
NSUB = QSTEP // TQ
TK = 512
ROWS = R * TQ
QK_AHEAD = 4
FAR_DIST = 2897
DELTA_MAX = -(-(FAR_DIST + TK - 1) // LANES) * LANES
STRIP_W = -(-(DELTA_MAX + QSTEP) // 256) * 256
WIN_KEYS = WINDOW + TQ
WIN_STRIP_W = -(-(WINDOW + WIN_KEYS) // 256) * 256
VMEM_LIMIT = 56 * 1024 * 1024


def _cparams(n_axes):
    return pltpu.CompilerParams(dimension_semantics=("arbitrary",) * n_axes,
                                vmem_limit_bytes=VMEM_LIMIT)


def _dot(a, b):
    return jnp.dot(a, b, preferred_element_type=F32)


def _split3(x):
    parts = []
    rem = x
    for _ in range(3):
        hi = rem.astype(BF16)
        parts.append(hi)
        rem = rem - hi.astype(F32)
    return parts


def _strip_kernel(tab_ref, o_ref, *, c0, dlimit, ch):
    g = pl.program_id(0)
    j = pl.program_id(1)
    span = 2 * ch
    shape = (8, span)
    p = lax.broadcasted_iota(jnp.int32, shape, 1)
    d = jnp.where(p < ch, p, p - span) + c0 - j * ch
    n = jnp.maximum(d, 0)
    max_exact = REL_BUCKETS // 2
    nf = jnp.maximum(n, max_exact).astype(F32)
    large = max_exact + (jnp.log(nf / max_exact) / math.log(REL_MAX_DIST / max_exact)
                         * (REL_BUCKETS - max_exact)).astype(jnp.int32)
    large = jnp.minimum(large, REL_BUCKETS - 1)
    bucket = jnp.where(n < max_exact, n, large)
    ok = jnp.where(d >= 0, jnp.where(d < dlimit, 1, 0), 0)
    for r in range(R):
        h = g * R + r
        val = jnp.zeros(shape, F32)
        for b in range(REL_BUCKETS):
            val = jnp.where(bucket == b, tab_ref[b, h], val)
        val = val - tab_ref[REL_BUCKETS - 1, h]
        w = jnp.where(ok > 0, val * LOG2E, NEG)
        wb = jnp.broadcast_to(w[0:1, :], (ch, span))
        o_ref[:, r * TQ:(r + 1) * TQ] = pltpu.roll(wb, 0, 1, stride=1, stride_axis=0)[:, :TQ].astype(o_ref.dtype)


def _bias_strip(rel_table, width, c0, dlimit):
    ch = 256
    return pl.pallas_call(
        functools.partial(_strip_kernel, c0=c0, dlimit=dlimit, ch=ch),
        grid=(N_GROUPS, width // ch),
        in_specs=[pl.BlockSpec(memory_space=pltpu.SMEM)],
        out_specs=pl.BlockSpec((None, ch, ROWS), lambda g, j: (g, j, 0)),
        out_shape=jax.ShapeDtypeStruct((N_GROUPS, width, ROWS), BF16),
        compiler_params=_cparams(2),
    )(rel_table.astype(F32))


def _head_norm(y, bd_ref, gain):
    outs = []
    for c in range(y.shape[1] // LANES):
        yc = y[:, c * LANES:(c + 1) * LANES]
        ss = _dot((yc * yc).astype(BF16), bd_ref[...])
        outs.append(yc * lax.rsqrt(ss * (1.0 / HEAD_DIM) + RMS_EPS))
    return jnp.concatenate(outs, axis=1) * gain


def _store_heads_t(q_ref, y):
    yt = y.T
    row = lax.broadcasted_iota(jnp.int32, (LANES, y.shape[0]), 0)
    for h in range(N_HEADS):
        pair = yt[(h // 2) * LANES:(h // 2 + 1) * LANES, :]
        keep = row < HEAD_DIM if h % 2 == 0 else row >= HEAD_DIM
        q_ref[h] = jnp.where(keep, pair, 0.0).astype(q_ref.dtype)


def _store_groups_dup(o_ref, y):
    lane = lax.broadcasted_iota(jnp.int32, (y.shape[0], LANES), 1)
    for g in range(N_GROUPS):
        pair = y[:, (g // 2) * LANES:(g // 2 + 1) * LANES]
        keep = lane < HEAD_DIM if g % 2 == 0 else lane >= HEAD_DIM
        m = jnp.where(keep, pair, 0.0)
        o_ref[g] = (m + pltpu.roll(m, HEAD_DIM, 1)).astype(o_ref.dtype)


def _store_groups_t(o_ref, y):
    yt = y.T
    ones = jnp.ones((V_ROWS - HEAD_DIM, y.shape[0]), o_ref.dtype)
    for g in range(N_GROUPS):
        o_ref[g, 0:HEAD_DIM, :] = yt[g * HEAD_DIM:(g + 1) * HEAD_DIM, :].astype(o_ref.dtype)
        o_ref[g, HEAD_DIM:V_ROWS, :] = ones


def _nsa_proj_kernel(x_ref, gn_ref, w_ref, bd_ref, qg_ref, kg_ref,
                     q_ref, cmp_ref, ksel_ref, vsel_ref, kwin_ref, vwin_ref, gate_ref):
    x = x_ref[...]
    ms = jnp.mean(x * x, axis=-1, keepdims=True)
    xn = (x * lax.rsqrt(ms + RMS_EPS) * gn_ref[...]).astype(BF16)
    d = N_HEADS * HEAD_DIM
    gd = N_GROUPS * HEAD_DIM
    yq = _dot(xn, w_ref[:, 0:d])
    _store_heads_t(q_ref, _head_norm(yq, bd_ref, qg_ref[...]))
    ycmp = _dot(xn, w_ref[:, d:d + 2 * gd])
    for c in range(2 * gd // LANES):
        cmp_ref[c] = ycmp[:, c * LANES:(c + 1) * LANES]
    ysel = _dot(xn, w_ref[:, d + 2 * gd:d + 4 * gd])
    _store_groups_dup(ksel_ref, _head_norm(ysel[:, :gd], bd_ref, kg_ref[0:1, :]))
    _store_groups_t(vsel_ref, ysel[:, gd:])
    ywin = _dot(xn, w_ref[:, d + 4 * gd:d + 6 * gd])
    _store_groups_dup(kwin_ref, _head_norm(ywin[:, :gd], bd_ref, kg_ref[1:2, :]))
    _store_groups_t(vwin_ref, ywin[:, gd:])
    yg = _dot(xn, w_ref[:, d + 6 * gd:])
    sgt = (1.0 / (1.0 + jnp.exp(-yg))).T
    for g in range(N_GROUPS):
        gate_ref[g] = sgt[g * LANES:g * LANES + GATE_ROWS, :]


def _qk_side_inputs(q_gain):
    bd = jnp.kron(jnp.eye(2, dtype=F32), jnp.ones((HEAD_DIM, HEAD_DIM), F32)).astype(BF16)
    qg = jnp.tile(q_gain.astype(F32) * (HEAD_DIM ** -0.5 * LOG2E), N_HEADS)[None, :]
    return bd, qg


def _nsa_project(x2, gain, w_in, q_gain, k_gain, B, S):
    D = x2.shape[1]
    d = N_HEADS * HEAD_DIM
    gd = N_GROUPS * HEAD_DIM
    wg = w_in[:, d + 6 * gd:].reshape(D, N_GROUPS, 3 * R)
    wg = jnp.pad(wg, ((0, 0), (0, 0), (0, LANES - 3 * R))).reshape(D, N_GROUPS * LANES)
    w = jnp.concatenate([w_in[:, :d + 6 * gd], wg], axis=1).astype(BF16)
    ncol = w.shape[1]
    bd, qg = _qk_side_inputs(q_gain)
    kg = jnp.stack([jnp.tile(k_gain[1].astype(F32), N_GROUPS), jnp.tile(k_gain[2].astype(F32), N_GROUPS)])
    tm = 256
    nt = S // tm
    kspec = pl.BlockSpec((None, N_GROUPS, tm, LANES), lambda i: (i // nt, 0, i % nt, 0))
    vspec = pl.BlockSpec((None, N_GROUPS, V_ROWS, tm), lambda i: (i // nt, 0, 0, i % nt))
    kshape = jax.ShapeDtypeStruct((B, N_GROUPS, S, LANES), BF16)
    vshape = jax.ShapeDtypeStruct((B, N_GROUPS, V_ROWS, S), BF16)
    return pl.pallas_call(
        _nsa_proj_kernel,
        grid=(B * nt,),
        in_specs=[
            pl.BlockSpec((tm, D), lambda i: (i, 0)),
            pl.BlockSpec((1, D), lambda i: (0, 0)),
            pl.BlockSpec((D, ncol), lambda i: (0, 0)),
            pl.BlockSpec((LANES, LANES), lambda i: (0, 0)),
            pl.BlockSpec((1, d), lambda i: (0, 0)),
            pl.BlockSpec((2, gd), lambda i: (0, 0)),
        ],
        out_specs=[
            pl.BlockSpec((None, N_HEADS, LANES, tm), lambda i: (i // nt, 0, 0, i % nt)),
            pl.BlockSpec((2 * gd // LANES, tm, LANES), lambda i: (0, i, 0)),
            kspec, vspec, kspec, vspec,
            pl.BlockSpec((None, N_GROUPS, GATE_ROWS, tm), lambda i: (i // nt, 0, 0, i % nt)),
        ],
        out_shape=[
            jax.ShapeDtypeStruct((B, N_HEADS, LANES, S), BF16),
            jax.ShapeDtypeStruct((2 * gd // LANES, B * S, LANES), F32),
            kshape, vshape, kshape, vshape,
            jax.ShapeDtypeStruct((B, N_GROUPS, GATE_ROWS, S), F32),
        ],
        compiler_params=_cparams(1),
    )(x2, gain[None, :].astype(F32), w, bd, qg, kg)


def _moba_proj_kernel(x_ref, gq_ref, gkv_ref, wq_ref, wkv_ref, bd_ref, qg_ref, kg_ref,
                      q_ref, k_ref, v_ref, kmean_ref):
    x = x_ref[...]
    ms = jnp.mean(x * x, axis=-1, keepdims=True)
    xr = x * lax.rsqrt(ms + RMS_EPS)
    gd = N_GROUPS * HEAD_DIM
    yq = _dot((xr * gq_ref[...]).astype(BF16), wq_ref[...])
    _store_heads_t(q_ref, _head_norm(yq, bd_ref, qg_ref[...]))
    ykv = _dot((xr * gkv_ref[...]).astype(BF16), wkv_ref[...])
    kn = _head_norm(ykv[:, :gd], bd_ref, kg_ref[...])
    _store_groups_dup(k_ref, kn)
    _store_groups_t(v_ref, ykv[:, gd:])
    kmean_ref[...] = jnp.mean(kn, axis=0, keepdims=True)


def _moba_project(h2, g_mix, g_kv, w_q, kv_w, q_gain, k_gain, B, S):
    D = h2.shape[1]
    d = N_HEADS * HEAD_DIM
    gd = N_GROUPS * HEAD_DIM
    bd, qg = _qk_side_inputs(q_gain)
    kg = jnp.tile(k_gain.astype(F32), N_GROUPS)[None, :]
    tm = MOBA_BLOCK
    nt = S // tm
    return pl.pallas_call(
        _moba_proj_kernel,
        grid=(B * nt,),
        in_specs=[
            pl.BlockSpec((tm, D), lambda i: (i, 0)),
            pl.BlockSpec((1, D), lambda i: (0, 0)),
            pl.BlockSpec((1, D), lambda i: (0, 0)),
            pl.BlockSpec((D, d), lambda i: (0, 0)),
            pl.BlockSpec((D, 2 * gd), lambda i: (0, 0)),
            pl.BlockSpec((LANES, LANES), lambda i: (0, 0)),
            pl.BlockSpec((1, d), lambda i: (0, 0)),
            pl.BlockSpec((1, gd), lambda i: (0, 0)),
        ],
        out_specs=[
            pl.BlockSpec((None, N_HEADS, LANES, tm), lambda i: (i // nt, 0, 0, i % nt)),
            pl.BlockSpec((None, N_GROUPS, tm, LANES), lambda i: (i // nt, 0, i % nt, 0)),
            pl.BlockSpec((None, N_GROUPS, V_ROWS, tm), lambda i: (i // nt, 0, 0, i % nt)),
            pl.BlockSpec((None, 1, gd), lambda i: (i, 0, 0)),
        ],
        out_shape=[
            jax.ShapeDtypeStruct((B, N_HEADS, LANES, S), BF16),
            jax.ShapeDtypeStruct((B, N_GROUPS, S, LANES), BF16),
            jax.ShapeDtypeStruct((B, N_GROUPS, V_ROWS, S), BF16),
            jax.ShapeDtypeStruct((B * nt, 1, gd), F32),
        ],
        compiler_params=_cparams(1),
    )(h2, g_mix[None, :].astype(F32), g_kv[None, :].astype(F32), w_q.astype(BF16), kv_w.astype(BF16),
      bd, qg, kg)


def _compress_kernel(x_ref, pos_ref, w1_ref, w2_ref, kg_ref, kc_ref, vct_ref, *, n_chunk):
    t = pl.program_id(1)
    half = CMP_BLOCK // 2
    za = [[], []]
    zb = [[], []]
    for l in range(half):
        for p in range(2):
            xl = x_ref[p, pl.ds(l, n_chunk, stride=CMP_STRIDE), :]
            za[p].append((xl + pos_ref[l:l + 1, p * LANES:(p + 1) * LANES]).astype(BF16))
            zb[p].append((xl + pos_ref[half + l:half + l + 1, p * LANES:(p + 1) * LANES]).astype(BF16))
    za = [jnp.concatenate(z, axis=1) for z in za]
    zb = [jnp.concatenate(z, axis=1) for z in zb]
    for g in range(N_GROUPS):
        p, e = g // 2, g % 2
        first = _dot(za[p], w1_ref[0, e])
        second = _dot(zb[p], w1_ref[1, e])
        hid = first + pltpu.roll(second, n_chunk - 1, 0)
        hid = hid * (1.0 / (1.0 + jnp.exp(-hid)))
        out = _dot(hid.astype(BF16), w2_ref[...])

        @pl.when(t == 0)
        def _():
            normed = out * lax.rsqrt(jnp.mean(out * out, axis=-1, keepdims=True) + RMS_EPS) * kg_ref[...]
            kc_ref[g] = normed.astype(kc_ref.dtype)

        @pl.when(t == 1)
        def _():
            vct_ref[g] = out.T[:HEAD_DIM, :].astype(vct_ref.dtype)


def _nsa_compress(cmp_raw, cmp_pos, cmp_w1, cmp_w2, k_gain0, B, S):
    gd = N_GROUPS * HEAD_DIM
    half = CMP_BLOCK // 2
    n_chunk = S // CMP_STRIDE
    pos = jnp.tile(cmp_pos.astype(F32), (1, 1, N_GROUPS))
    w1 = cmp_w1.astype(F32).reshape(2, 2, half, HEAD_DIM, CMP_HIDDEN)
    z = jnp.zeros_like(w1)
    w1p = jnp.stack([jnp.concatenate([w1, z], axis=3), jnp.concatenate([z, w1], axis=3)], axis=2)
    w1p = w1p.reshape(2, 2, 2, half * LANES, CMP_HIDDEN).astype(BF16)
    w2 = jnp.concatenate([cmp_w2, cmp_w2], axis=-1).astype(BF16)
    kg = jnp.tile(k_gain0.astype(F32), 2)[None, :]
    return pl.pallas_call(
        functools.partial(_compress_kernel, n_chunk=n_chunk),
        grid=(B, 2),
        in_specs=[
            pl.BlockSpec((2, S, LANES), lambda b, t: (t, b, 0)),
            pl.BlockSpec((None, CMP_BLOCK, gd), lambda b, t: (t, 0, 0)),
            pl.BlockSpec((None, 2, 2, half * LANES, CMP_HIDDEN), lambda b, t: (t, 0, 0, 0, 0)),
            pl.BlockSpec((None, CMP_HIDDEN, LANES), lambda b, t: (t, 0, 0)),
            pl.BlockSpec((1, LANES), lambda b, t: (0, 0)),
        ],
        out_specs=[
            pl.BlockSpec((None, N_GROUPS, n_chunk, LANES), lambda b, t: (b, 0, 0, 0)),
            pl.BlockSpec((None, N_GROUPS, HEAD_DIM, n_chunk), lambda b, t: (b, 0, 0, 0)),
        ],
        out_shape=[
            jax.ShapeDtypeStruct((B, N_GROUPS, n_chunk, LANES), BF16),
            jax.ShapeDtypeStruct((B, N_GROUPS, HEAD_DIM, n_chunk), BF16),
        ],
        compiler_params=_cparams(2),
    )(cmp_raw, pos, w1p, w2, kg)


_Q_SPEC = pl.BlockSpec((None, R, LANES, QSTEP), lambda b, g, i: (b, g, 0, i))
_GATE_SPEC = pl.BlockSpec((None, None, GATE_ROWS, QSTEP), lambda b, g, i: (b, g, 0, i))
_O_SPEC = pl.BlockSpec((None, R * HEAD_DIM, QSTEP), lambda b, g, i: (b, g, i))


def _chains():
    return [(u, hf) for u in range(NSUB) for hf in range(R // 2)]


def _run_chains(scores, finish, ahead=None, next_scores=None):
    chains = _chains()
    ahead = [scores(ch) for ch in chains[:QK_AHEAD]] if ahead is None else list(ahead)
    for idx, ch in enumerate(chains):
        s = ahead.pop(0)
        nxt = idx + QK_AHEAD
        if nxt < len(chains):
            ahead.append(scores(chains[nxt]))
        elif next_scores is not None:
            ahead.append(next_scores(chains[nxt - len(chains)]))
        finish(ch, s)
    return ahead


def _chain_qt(q_ref, u, hf):
    return jnp.concatenate([q_ref[2 * hf + e, :, u * TQ:(u + 1) * TQ] for e in range(2)], axis=1)


def _store_chain_out(o_ref, ot, gate_ref, gate_col, u, hf):
    for e in range(2):
        r = 2 * hf + e
        o_r = ot[:, e * TQ:(e + 1) * TQ]
        if gate_ref is not None:
            c = 3 * r + gate_col
            o_r = o_r * gate_ref[c:c + 1, u * TQ:(u + 1) * TQ]
        o_ref[r * HEAD_DIM:(r + 1) * HEAD_DIM, u * TQ:(u + 1) * TQ] = o_r.astype(o_ref.dtype)


def _topk_mark_t(score, k):
    row = lax.broadcasted_iota(jnp.int32, score.shape, 0).astype(F32)
    s = score
    for _ in range(k):
        m = jnp.max(s, axis=0, keepdims=True)
        idx = jnp.min(jnp.where(s == m, row, float(LANES)), axis=0, keepdims=True)
        s = jnp.where(row == idx, -jnp.inf, s)
    return s


N_CHAINS = NSUB * (R // 2)
_FLASH_SCRATCH = [pltpu.VMEM((QK_AHEAD, TK, 2 * TQ), BF16), pltpu.VMEM((QK_AHEAD, 8, 2 * TQ), F32),
                  pltpu.VMEM((N_CHAINS, 8, 2 * TQ), F32), pltpu.VMEM((N_CHAINS, V_ROWS, 2 * TQ), F32)]


def _flash_tiles(qas, k_ref, oh_ref, vt_ref, strip_ref, scratch, q0, n_tiles):
    chains = _chains()
    cw = 2 * TQ
    last_k0 = k_ref.shape[0] - TK
    s_scr, mx_scr, m_scr, acc_scr = scratch

    def tile_scores(kt):
        k0 = pl.multiple_of(jnp.minimum(kt * TK, last_k0), TK)
        ka = jnp.concatenate([oh_ref[pl.ds(k0, TK), :], k_ref[pl.ds(k0, TK), :]], axis=1)

        def scores(ch):
            u, hf = ch
            cs = pl.multiple_of(DELTA_MAX - jnp.clip(q0 + u * TQ - k0, 0, DELTA_MAX), LANES)
            s = _dot(ka, qas[ch]).astype(BF16) + strip_ref[pl.ds(cs, TK), hf * cw:(hf + 1) * cw]
            return s, jnp.max(s, axis=0, keepdims=True).astype(F32)

        return scores

    def park(slot, s_and_max):
        s_scr[slot] = s_and_max[0]
        mx_scr[slot] = jnp.broadcast_to(s_and_max[1], mx_scr.shape[1:])

    def one_tile(kt):
        vt = vt_ref[:, pl.ds(pl.multiple_of(kt * TK, TK), TK)]
        cur, nxt = tile_scores(kt), tile_scores(kt + 1)
        in_flight = {}
        for idx, ch in enumerate(chains):
            s, s_max = (s_scr[idx], mx_scr[idx, 0:1]) if idx < QK_AHEAD else in_flight.pop(idx)
            ahead = idx + QK_AHEAD
            if ahead < len(chains):
                in_flight[ahead] = cur(chains[ahead])
            else:
                park(ahead - len(chains), nxt(chains[ahead - len(chains)]))
            m = m_scr[idx, 0:1]
            m_new = jnp.maximum(m, s_max)
            p = jnp.exp2(s - m_new.astype(BF16))
            acc_scr[idx] = jnp.exp2(m - m_new) * acc_scr[idx] + _dot(vt, p)
            m_scr[idx] = jnp.broadcast_to(m_new, m_scr.shape[1:])

    def two_tiles(it, _):
        one_tile(2 * it)
        one_tile(2 * it + 1)
        return 0

    first = tile_scores(0)
    for slot in range(QK_AHEAD):
        park(slot, first(chains[slot]))
    m_scr[...] = jnp.full(m_scr.shape, 3 * NEG, F32)
    acc_scr[...] = jnp.zeros(acc_scr.shape, F32)
    lax.fori_loop(0, n_tiles // 2, two_tiles, 0)
    pl.when(n_tiles % 2 == 1)(lambda: one_tile(n_tiles - 1))
    return {ch: acc_scr[idx, 0:HEAD_DIM] * (1.0 / acc_scr[idx, HEAD_DIM:HEAD_DIM + 1])
            for idx, ch in enumerate(chains)}


def _nsa_cmp_kernel(q_ref, kc_ref, vct_ref, vis_ref, c2st_ref, gate_ref, o_ref, mb_ref, imp_scr, *, n_chunk):
    q0 = pl.program_id(2) * QSTEP
    cw = 2 * TQ
    i_col = lax.broadcasted_iota(jnp.int32, (1, cw), 1) & (TQ - 1)

    def attend(n_used):
        psums = [None] * NSUB

        def scores(ch):
            r0 = pl.multiple_of(n_chunk - (q0 + ch[0] * TQ) // CMP_STRIDE, 8)
            return _dot(kc_ref[0:n_used, :], _chain_qt(q_ref, *ch)) + vis_ref[pl.ds(r0, n_used), :]

        def finish(ch, lcm):
            u, hf = ch
            e = jnp.exp2(lcm - jnp.max(lcm, axis=0, keepdims=True))
            any_visible = q0 + u * TQ + i_col >= CMP_BLOCK - 1
            pc = e * jnp.where(any_visible, 1.0 / jnp.sum(e, axis=0, keepdims=True), 0.0)
            oc = _dot(vct_ref[:, 0:n_used], pc.astype(BF16))
            _store_chain_out(o_ref, oc, gate_ref, 0, u, hf)
            ps = pc[:, :TQ] + pc[:, TQ:]
            psums[u] = ps if psums[u] is None else psums[u] + ps

        _run_chains(scores, finish)
        psum = jnp.concatenate(psums, axis=1)
        imp = None
        for part in _split3(psum):
            term = _dot(c2st_ref[:, 0:n_used], part)
            imp = term if imp is None else imp + term
        imp_scr[...] = imp

    span = LANES * CMP_STRIDE
    for v in range(1, n_chunk // LANES + 1):
        pl.when(q0 // span + 1 == v)(functools.partial(attend, LANES * v))

    imp = imp_scr[...]
    cur = (q0 + lax.broadcasted_iota(jnp.int32, (LANES, QSTEP), 1)) // SEL_BLOCK
    j = lax.broadcasted_iota(jnp.int32, (LANES, QSTEP), 0)
    forced_or_imp = jnp.where(j == 0, -jnp.inf, jnp.where(j >= cur - 1, -jnp.inf, imp))
    score = jnp.where(j <= cur, forced_or_imp, -BIG)
    chosen = _topk_mark_t(score, SEL_TOPN - N_FORCED)
    mb_ref[...] = jnp.where(chosen == -jnp.inf, 0.0, NEG).astype(mb_ref.dtype)


def _nsa_cmp(qt, kc, vct, gates, B, S):
    n_chunk = S // CMP_STRIDE
    n_sel = S // SEL_BLOCK
    assert n_sel <= LANES and n_chunk % LANES == 0
    cs = jnp.arange(n_chunk)[None, :] * CMP_STRIDE
    ss = jnp.arange(LANES)[:, None] * SEL_BLOCK
    overlap = jnp.clip(jnp.minimum(cs + CMP_BLOCK, ss + SEL_BLOCK) - jnp.maximum(cs, ss), 0, None) / CMP_BLOCK
    n_cmp = (S - CMP_BLOCK) // CMP_STRIDE + 1
    c2st = jnp.where((jnp.arange(n_chunk)[None, :] < n_cmp) & (jnp.arange(LANES)[:, None] < n_sel),
                     overlap, 0.0).astype(BF16)
    c_end = (jnp.arange(2 * n_chunk)[:, None] - n_chunk) * CMP_STRIDE + (CMP_BLOCK - 1)
    vis = jnp.where(c_end <= (jnp.arange(2 * TQ)[None, :] & (TQ - 1)), 0.0, NEG).astype(F32)
    return pl.pallas_call(
        functools.partial(_nsa_cmp_kernel, n_chunk=n_chunk),
        grid=(B, N_GROUPS, S // QSTEP),
        in_specs=[
            _Q_SPEC,
            pl.BlockSpec((None, None, n_chunk, LANES), lambda b, g, i: (b, g, 0, 0)),
            pl.BlockSpec((None, None, HEAD_DIM, n_chunk), lambda b, g, i: (b, g, 0, 0)),
            pl.BlockSpec((2 * n_chunk, 2 * TQ), lambda b, g, i: (0, 0)),
            pl.BlockSpec((LANES, n_chunk), lambda b, g, i: (0, 0)),
            _GATE_SPEC,
        ],
        out_specs=[
            _O_SPEC,
            pl.BlockSpec((None, None, LANES, QSTEP), lambda b, g, i: (b, g, 0, i)),
        ],
        out_shape=[
            jax.ShapeDtypeStruct((B, N_HEADS * HEAD_DIM, S), ATTN_OUT_DTYPE),
            jax.ShapeDtypeStruct((B, N_GROUPS, LANES, S), BF16),
        ],
        scratch_shapes=[pltpu.VMEM((LANES, QSTEP), F32)],
        compiler_params=_cparams(3),
    )(qt, kc, vct, vis, c2st, gates)


def _nsa_sel_kernel(q_ref, mb_ref, k_ref, oh_ref, vt_ref, strip_ref, gate_ref, o_ref, *scratch):
    q0 = pl.program_id(2) * QSTEP
    qas = {}
    for u, hf in _chains():
        mb = mb_ref[:, u * TQ:(u + 1) * TQ]
        qas[(u, hf)] = jnp.concatenate([jnp.concatenate([mb, mb], axis=1), _chain_qt(q_ref, u, hf)], axis=0)
    n_tiles = (q0 + QSTEP) // TK
    o = _flash_tiles(qas, k_ref, oh_ref, vt_ref, strip_ref, scratch, q0, n_tiles)
    for u, hf in _chains():
        _store_chain_out(o_ref, o[(u, hf)], gate_ref, 1, u, hf)


def _nsa_sel(qt, mb, ksel, vsel_t, strip, gates, B, S):
    onehot = (jnp.arange(S)[:, None] // SEL_BLOCK == jnp.arange(LANES)[None, :]).astype(BF16)
    return pl.pallas_call(
        _nsa_sel_kernel,
        grid=(B, N_GROUPS, S // QSTEP),
        in_specs=[
            _Q_SPEC,
            pl.BlockSpec((None, None, LANES, QSTEP), lambda b, g, i: (b, g, 0, i)),
            pl.BlockSpec((None, None, S, LANES), lambda b, g, i: (b, g, 0, 0)),
            pl.BlockSpec((S, LANES), lambda b, g, i: (0, 0)),
            pl.BlockSpec((None, None, V_ROWS, S), lambda b, g, i: (b, g, 0, 0)),
            pl.BlockSpec((None, STRIP_W, ROWS), lambda b, g, i: (g, 0, 0)),
            _GATE_SPEC,
        ],
        out_specs=_O_SPEC,
        out_shape=jax.ShapeDtypeStruct((B, N_HEADS * HEAD_DIM, S), ATTN_OUT_DTYPE),
        scratch_shapes=_FLASH_SCRATCH,
        compiler_params=_cparams(3),
    )(qt, mb, ksel, onehot, vsel_t, strip, gates)


def _nsa_win_kernel(q_ref, k_ref, vt_ref, strip_ref, gate_ref, o_ref):
    q0 = pl.program_id(2) * QSTEP
    cw = 2 * TQ

    def key_start(u):
        return pl.multiple_of(jnp.maximum(q0 + u * TQ - WINDOW, 0), TQ)

    def scores(ch):
        u, hf = ch
        k0 = key_start(u)
        cs = pl.multiple_of(WINDOW - (q0 + u * TQ - k0), LANES)
        return (_dot(k_ref[pl.ds(k0, WIN_KEYS), :], _chain_qt(q_ref, u, hf)).astype(BF16)
                + strip_ref[pl.ds(cs, WIN_KEYS), hf * cw:(hf + 1) * cw])

    def finish(ch, s):
        u, hf = ch
        e = jnp.exp2(s - jnp.max(s, axis=0, keepdims=True))
        o = _dot(vt_ref[:, pl.ds(key_start(u), WIN_KEYS)], e)
        o = o[:HEAD_DIM] * (1.0 / o[HEAD_DIM:HEAD_DIM + 1])
        _store_chain_out(o_ref, o, gate_ref, 2, u, hf)

    _run_chains(scores, finish)


def _nsa_win(qt, kwin, vwin_t, strip, gates, B, S):
    assert S >= WIN_KEYS
    return pl.pallas_call(
        _nsa_win_kernel,
        grid=(B, N_GROUPS, S // QSTEP),
        in_specs=[
            _Q_SPEC,
            pl.BlockSpec((None, None, S, LANES), lambda b, g, i: (b, g, 0, 0)),
            pl.BlockSpec((None, None, V_ROWS, S), lambda b, g, i: (b, g, 0, 0)),
            pl.BlockSpec((None, WIN_STRIP_W, ROWS), lambda b, g, i: (g, 0, 0)),
            _GATE_SPEC,
        ],
        out_specs=_O_SPEC,
        out_shape=jax.ShapeDtypeStruct((B, N_HEADS * HEAD_DIM, S), ATTN_OUT_DTYPE),
        compiler_params=_cparams(3),
    )(qt, kwin, vwin_t, strip, gates)


def _moba_kernel(q_ref, kmean_ref, k_ref, oh_ref, vt_ref, strip_ref, o_ref, *scratch):
    q0 = pl.program_id(2) * QSTEP
    cw = 2 * TQ
    nb = kmean_ref.shape[0]
    n = lax.broadcasted_iota(jnp.int32, (nb, cw), 0)
    no_block = jnp.full((LANES - nb, cw), NEG, BF16)
    qas = {}
    for u, hf in _chains():
        cblk = (q0 + u * TQ) // MOBA_BLOCK
        qt = _chain_qt(q_ref, u, hf)
        gate = _dot(kmean_ref[...], qt)
        past = n < cblk
        chosen = _topk_mark_t(jnp.where(past, gate, NEG), MOBA_TOPK)
        mb = jnp.where(n == cblk, 0.0, jnp.where(past, jnp.where(chosen == -jnp.inf, 0.0, NEG), NEG)).astype(BF16)
        qas[(u, hf)] = jnp.concatenate([mb, no_block, qt], axis=0)
    n_tiles = (q0 + QSTEP) // TK
    o = _flash_tiles(qas, k_ref, oh_ref, vt_ref, strip_ref, scratch, q0, n_tiles)
    for u, hf in _chains():
        _store_chain_out(o_ref, o[(u, hf)], None, 0, u, hf)


def _moba_attn(qt, kmean, k, vt, strip, B, S):
    n_blk = S // MOBA_BLOCK
    assert n_blk <= LANES
    onehot = (jnp.arange(S)[:, None] // MOBA_BLOCK == jnp.arange(LANES)[None, :]).astype(BF16)
    return pl.pallas_call(
        _moba_kernel,
        grid=(B, N_GROUPS, S // QSTEP),
        in_specs=[
            _Q_SPEC,
            pl.BlockSpec((None, None, kmean.shape[2], LANES), lambda b, g, i: (b, g, 0, 0)),
            pl.BlockSpec((None, None, S, LANES), lambda b, g, i: (b, g, 0, 0)),
            pl.BlockSpec((S, LANES), lambda b, g, i: (0, 0)),
            pl.BlockSpec((None, None, V_ROWS, S), lambda b, g, i: (b, g, 0, 0)),
            pl.BlockSpec((None, STRIP_W, ROWS), lambda b, g, i: (g, 0, 0)),
        ],
        out_specs=_O_SPEC,
        out_shape=jax.ShapeDtypeStruct((B, N_HEADS * HEAD_DIM, S), ATTN_OUT_DTYPE),
        scratch_shapes=_FLASH_SCRATCH,
        compiler_params=_cparams(3),
    )(qt, kmean, k, onehot, vt, strip)


def _out_mlp_kernel(*refs, ff_chunk):
    *o_refs, wo_ref, h_ref, g_ref, wu_ref, wd_ref, out_ref = refs
    o = o_refs[0][...]
    if len(o_refs) > 1:
        o = o.astype(F32)
        for r in o_refs[1:]:
            o = o + r[...].astype(F32)
    h = h_ref[...] + lax.dot_general(o.astype(BF16), wo_ref[...], (((0,), (0,)), ((), ())),
                                     preferred_element_type=F32)
    ms = jnp.mean(h * h, axis=-1, keepdims=True)
    xn = (h * lax.rsqrt(ms + RMS_EPS) * g_ref[...]).astype(BF16)
    acc = h
    for c in range(wu_ref.shape[1] // ff_chunk):
        a = jnp.maximum(_dot(xn, wu_ref[:, c * ff_chunk:(c + 1) * ff_chunk]), 0.0)
        acc = acc + _dot((a * a).astype(BF16), wd_ref[c * ff_chunk:(c + 1) * ff_chunk, :])
    out_ref[...] = acc


def _out_proj_mlp(parts_t, w_out, h2, gain, w_up, w_down, B, S):
    d = parts_t[0].shape[1]
    D = w_out.shape[1]
    F = w_up.shape[1]
    tm = 256
    nt = S // tm
    const = lambda i: (0, 0)
    return pl.pallas_call(
        functools.partial(_out_mlp_kernel, ff_chunk=1024),
        grid=(B * nt,),
        in_specs=[pl.BlockSpec((None, d, tm), lambda i: (i // nt, 0, i % nt)) for _ in parts_t] + [
            pl.BlockSpec((d, D), const),
            pl.BlockSpec((tm, D), lambda i: (i, 0)),
            pl.BlockSpec((1, D), const),
            pl.BlockSpec((D, F), const),
            pl.BlockSpec((F, D), const),
        ],
        out_specs=pl.BlockSpec((tm, D), lambda i: (i, 0)),
        out_shape=jax.ShapeDtypeStruct((B * S, D), F32),
        compiler_params=_cparams(1),
    )(*parts_t, w_out.astype(BF16), h2, gain[None, :].astype(F32), w_up.astype(BF16), w_down.astype(BF16))


def _nsa_mixer(h2, B, S, norm_mix, w_in, q_gain, k_gain, cmp_pos, cmp_w1, cmp_w2, sel_strip, win_strip):
    qt, cmp_raw, ksel, vsel_t, kwin, vwin_t, gates = _nsa_project(h2, norm_mix, w_in, q_gain, k_gain, B, S)
    kc, vct = _nsa_compress(cmp_raw, cmp_pos, cmp_w1, cmp_w2, k_gain[0], B, S)
    o_cmp, mb = _nsa_cmp(qt, kc, vct, gates, B, S)
    o_sel = _nsa_sel(qt, mb, ksel, vsel_t, sel_strip, gates, B, S)
    o_win = _nsa_win(qt, kwin, vwin_t, win_strip, gates, B, S)
    return [o_cmp, o_sel, o_win]


def _shared_kv_and_q(h2, B, S, norm_mix, kv_norm, kv_w, kv_k_gain, w_q, q_gain):
    qt, k, vt, kmean = _moba_project(h2, norm_mix, kv_norm, w_q, kv_w, q_gain, kv_k_gain, B, S)
    n_blk = S // MOBA_BLOCK
    km = kmean.reshape(B, n_blk, N_GROUPS, HEAD_DIM).transpose(0, 2, 1, 3)
    km = jnp.concatenate([km, km], axis=-1)
    km = jnp.pad(km, ((0, 0), (0, 0), (0, -n_blk % 16), (0, 0))).astype(BF16)
    return qt, k, vt, km


def kernel(x, norm_mix, norm_mlp, nsa_w_in, nsa_q_gain, nsa_k_gain, nsa_cmp_pos, nsa_cmp_w1, nsa_cmp_w2,
           nsa_w_out, kv_norm, kv_w, kv_k_gain, moba_w_q, moba_q_gain, moba_w_out, rel_table, mlp_w_up,
           mlp_w_down):
    B, S, D = x.shape
    depth = norm_mix.shape[0]
    n_a = nsa_w_in.shape[0]
    assert S % TK == 0 and S % MOBA_BLOCK == 0
    sel_strip = _bias_strip(rel_table, STRIP_W, DELTA_MAX, 1 << 30)
    win_strip = _bias_strip(rel_table, WIN_STRIP_W, WINDOW, WINDOW)
    h2 = x.reshape(B * S, D)
    shared = None
    for layer in range(depth):
        if layer < n_a:
            i = layer
            parts = _nsa_mixer(h2, B, S, norm_mix[layer], nsa_w_in[i], nsa_q_gain[i], nsa_k_gain[i],
                               nsa_cmp_pos[i], nsa_cmp_w1[i], nsa_cmp_w2[i], sel_strip, win_strip)
            w_out = nsa_w_out[i]
        else:
            j = layer - n_a
            if shared is None:
                qt, k, vt, km = _shared_kv_and_q(h2, B, S, norm_mix[layer], kv_norm, kv_w, kv_k_gain,
                                                 moba_w_q[j], moba_q_gain[j])
                shared = (k, vt, km)
            else:
                qt = _moba_project(h2, norm_mix[layer], kv_norm, moba_w_q[j], kv_w, moba_q_gain[j],
                                   kv_k_gain, B, S)[0]
            k, vt, km = shared
            parts = [_moba_attn(qt, km, k, vt, sel_strip, B, S)]
            w_out = moba_w_out[j]
        h2 = _out_proj_mlp(parts, w_out, h2, norm_mlp[layer], mlp_w_up[layer], mlp_w_down[layer], B, S)
    return h2.reshape(B, S, D)
```

```python
import functools
import math

import jax
import jax.numpy as jnp
from jax import lax
from jax.experimental import pallas as pl
from jax.experimental.pallas import tpu as pltpu

F32 = jnp.float32
BF16 = jnp.bfloat16

N_HEADS = 16
HEAD_DIM = 64
N_GROUPS = 4
R = N_HEADS // N_GROUPS
LANES = 128
CMP_BLOCK = 32
CMP_STRIDE = 16
CMP_HIDDEN = 4 * HEAD_DIM
SEL_BLOCK = 64
SEL_TOPN = 16
N_FORCED = 3
WINDOW = 512
MOBA_BLOCK = 256
MOBA_TOPK = 3
REL_BUCKETS = 32
REL_MAX_DIST = 4096
RMS_EPS = 1e-6
NEG = -1e30
BIG = 1e9
LOG2E = math.log2(math.e)
NORM_BLOCK = 256
GATE_ROWS = 16
ATTN_OUT_DTYPE = BF16
V_ROWS = HEAD_DIM + 16

TQ = 128
QSTEP = 512
NSUB = QSTEP // TQ
TK = 512
ROWS = R * TQ
QK_AHEAD = 4
FAR_DIST = 2897
DELTA_MAX = -(-(FAR_DIST + TK - 1) // LANES) * LANES
STRIP_W = -(-(DELTA_MAX + QSTEP) // 256) * 256
WIN_KEYS = WINDOW + TQ
WIN_STRIP_W = -(-(WINDOW + WIN_KEYS) // 256) * 256
VMEM_LIMIT = 56 * 1024 * 1024


def _cparams(n_axes):
    return pltpu.CompilerParams(dimension_semantics=("arbitrary",) * n_axes,
                                vmem_limit_bytes=VMEM_LIMIT)


def _dot(a, b):
    return jnp.dot(a, b, preferred_element_type=F32)


def _split3(x):
    parts = []
    rem = x
    for _ in range(3):
        hi = rem.astype(BF16)
        parts.append(hi)
        rem = rem - hi.astype(F32)
    return parts


def _strip_kernel(tab_ref, o_ref, *, c0, dlimit, ch):
    g = pl.program_id(0)
    j = pl.program_id(1)
    span = 2 * ch
    shape = (8, span)
    p = lax.broadcasted_iota(jnp.int32, shape, 1)
    d = jnp.where(p < ch, p, p - span) + c0 - j * ch
    n = jnp.maximum(d, 0)
    max_exact = REL_BUCKETS // 2
    nf = jnp.maximum(n, max_exact).astype(F32)
    large = max_exact + (jnp.log(nf / max_exact) / math.log(REL_MAX_DIST / max_exact)
                         * (REL_BUCKETS - max_exact)).astype(jnp.int32)
    large = jnp.minimum(large, REL_BUCKETS - 1)
    bucket = jnp.where(n < max_exact, n, large)
    ok = jnp.where(d >= 0, jnp.where(d < dlimit, 1, 0), 0)
    for r in range(R):
        h = g * R + r
        val = jnp.zeros(shape, F32)
        for b in range(REL_BUCKETS):
            val = jnp.where(bucket == b, tab_ref[b, h], val)
        val = val - tab_ref[REL_BUCKETS - 1, h]
        w = jnp.where(ok > 0, val * LOG2E, NEG)
        wb = jnp.broadcast_to(w[0:1, :], (ch, span))
        o_ref[:, r * TQ:(r + 1) * TQ] = pltpu.roll(wb, 0, 1, stride=1, stride_axis=0)[:, :TQ].astype(o_ref.dtype)


def _bias_strip(rel_table, width, c0, dlimit):
    ch = 256
    return pl.pallas_call(
        functools.partial(_strip_kernel, c0=c0, dlimit=dlimit, ch=ch),
        grid=(N_GROUPS, width // ch),
        in_specs=[pl.BlockSpec(memory_space=pltpu.SMEM)],
        out_specs=pl.BlockSpec((None, ch, ROWS), lambda g, j: (g, j, 0)),
        out_shape=jax.ShapeDtypeStruct((N_GROUPS, width, ROWS), BF16),
        compiler_params=_cparams(2),
    )(rel_table.astype(F32))


def _head_norm(y, bd_ref, gain):
    outs = []
    bw = bd_ref.shape[0]
    for c in range(y.shape[1] // bw):
        yc = y[:, c * bw:(c + 1) * bw]
        ss = _dot((yc * yc).astype(BF16), bd_ref[...])
        outs.append(yc * lax.rsqrt(ss * (1.0 / HEAD_DIM) + RMS_EPS))
    return jnp.concatenate(outs, axis=1) * gain


def _store_heads_t(q_ref, y):
    yt = y.T
    row = lax.broadcasted_iota(jnp.int32, (LANES, y.shape[0]), 0)
    for h in range(N_HEADS):
        pair = yt[(h // 2) * LANES:(h // 2 + 1) * LANES, :]
        keep = row < HEAD_DIM if h % 2 == 0 else row >= HEAD_DIM
        q_ref[h] = jnp.where(keep, pair, 0.0).astype(q_ref.dtype)


def _store_groups_dup(o_ref, y):
    lane = lax.broadcasted_iota(jnp.int32, (y.shape[0], LANES), 1)
    for g in range(N_GROUPS):
        pair = y[:, (g // 2) * LANES:(g // 2 + 1) * LANES]
        keep = lane < HEAD_DIM if g % 2 == 0 else lane >= HEAD_DIM
        m = jnp.where(keep, pair, 0.0)
        o_ref[g] = (m + pltpu.roll(m, HEAD_DIM, 1)).astype(o_ref.dtype)


def _store_groups_t(o_ref, y):
    yt = y.T
    ones = jnp.ones((V_ROWS - HEAD_DIM, y.shape[0]), o_ref.dtype)
    for g in range(N_GROUPS):
        o_ref[g, 0:HEAD_DIM, :] = yt[g * HEAD_DIM:(g + 1) * HEAD_DIM, :].astype(o_ref.dtype)
        o_ref[g, HEAD_DIM:V_ROWS, :] = ones


def _nsa_proj_kernel(x_ref, gn_ref, w_ref, bd_ref, qg_ref, kg_ref,
                     q_ref, cmp_ref, ksel_ref, vsel_ref, kwin_ref, vwin_ref, gate_ref):
    x = x_ref[...]
    ms = jnp.mean(x * x, axis=-1, keepdims=True)
    xn = (x * lax.rsqrt(ms + RMS_EPS) * gn_ref[...]).astype(BF16)
    d = N_HEADS * HEAD_DIM
    gd = N_GROUPS * HEAD_DIM
    yq = _dot(xn, w_ref[:, 0:d])
    _store_heads_t(q_ref, _head_norm(yq, bd_ref, qg_ref[...]))
    ycmp = _dot(xn, w_ref[:, d:d + 2 * gd])
    for c in range(2 * gd // LANES):
        cmp_ref[c] = ycmp[:, c * LANES:(c + 1) * LANES]
    ysel = _dot(xn, w_ref[:, d + 2 * gd:d + 4 * gd])
    _store_groups_dup(ksel_ref, _head_norm(ysel[:, :gd], bd_ref, kg_ref[0:1, :]))
    _store_groups_t(vsel_ref, ysel[:, gd:])
    ywin = _dot(xn, w_ref[:, d + 4 * gd:d + 6 * gd])
    _store_groups_dup(kwin_ref, _head_norm(ywin[:, :gd], bd_ref, kg_ref[1:2, :]))
    _store_groups_t(vwin_ref, ywin[:, gd:])
    yg = _dot(xn, w_ref[:, d + 6 * gd:])
    sgt = (1.0 / (1.0 + jnp.exp(-yg))).T
    for g in range(N_GROUPS):
        gate_ref[g] = sgt[g * LANES:g * LANES + GATE_ROWS, :]


def _qk_side_inputs(q_gain):
    bd = jnp.kron(jnp.eye(NORM_BLOCK // HEAD_DIM, dtype=F32), jnp.ones((HEAD_DIM, HEAD_DIM), F32)).astype(BF16)
    qg = jnp.tile(q_gain.astype(F32) * (HEAD_DIM ** -0.5 * LOG2E), N_HEADS)[None, :]
    return bd, qg


def _nsa_project(x2, gain, w_in, q_gain, k_gain, B, S):
    D = x2.shape[1]
    d = N_HEADS * HEAD_DIM
    gd = N_GROUPS * HEAD_DIM
    wg = w_in[:, d + 6 * gd:].reshape(D, N_GROUPS, 3 * R)
    wg = jnp.pad(wg, ((0, 0), (0, 0), (0, LANES - 3 * R))).reshape(D, N_GROUPS * LANES)
    w = jnp.concatenate([w_in[:, :d + 6 * gd], wg], axis=1).astype(BF16)
    ncol = w.shape[1]
    bd, qg = _qk_side_inputs(q_gain)
    kg = jnp.stack([jnp.tile(k_gain[1].astype(F32), N_GROUPS), jnp.tile(k_gain[2].astype(F32), N_GROUPS)])
    tm = 256
    nt = S // tm
    kspec = pl.BlockSpec((None, N_GROUPS, tm, LANES), lambda i: (i // nt, 0, i % nt, 0))
    vspec = pl.BlockSpec((None, N_GROUPS, V_ROWS, tm), lambda i: (i // nt, 0, 0, i % nt))
    kshape = jax.ShapeDtypeStruct((B, N_GROUPS, S, LANES), BF16)
    vshape = jax.ShapeDtypeStruct((B, N_GROUPS, V_ROWS, S), BF16)
    return pl.pallas_call(
        _nsa_proj_kernel,
        grid=(B * nt,),
        in_specs=[
            pl.BlockSpec((tm, D), lambda i: (i, 0)),
            pl.BlockSpec((1, D), lambda i: (0, 0)),
            pl.BlockSpec((D, ncol), lambda i: (0, 0)),
            pl.BlockSpec((NORM_BLOCK, NORM_BLOCK), lambda i: (0, 0)),
            pl.BlockSpec((1, d), lambda i: (0, 0)),
            pl.BlockSpec((2, gd), lambda i: (0, 0)),
        ],
        out_specs=[
            pl.BlockSpec((None, N_HEADS, LANES, tm), lambda i: (i // nt, 0, 0, i % nt)),
            pl.BlockSpec((2 * gd // LANES, tm, LANES), lambda i: (0, i, 0)),
            kspec, vspec, kspec, vspec,
            pl.BlockSpec((None, N_GROUPS, GATE_ROWS, tm), lambda i: (i // nt, 0, 0, i % nt)),
        ],
        out_shape=[
            jax.ShapeDtypeStruct((B, N_HEADS, LANES, S), BF16),
            jax.ShapeDtypeStruct((2 * gd // LANES, B * S, LANES), F32),
            kshape, vshape, kshape, vshape,
            jax.ShapeDtypeStruct((B, N_GROUPS, GATE_ROWS, S), F32),
        ],
        compiler_params=_cparams(1),
    )(x2, gain[None, :].astype(F32), w, bd, qg, kg)


def _moba_proj_kernel(x_ref, gq_ref, gkv_ref, wq_ref, wkv_ref, bd_ref, qg_ref, kg_ref,
                      q_ref, k_ref, v_ref, kmean_ref):
    x = x_ref[...]
    ms = jnp.mean(x * x, axis=-1, keepdims=True)
    xr = x * lax.rsqrt(ms + RMS_EPS)
    gd = N_GROUPS * HEAD_DIM
    yq = _dot((xr * gq_ref[...]).astype(BF16), wq_ref[...])
    _store_heads_t(q_ref, _head_norm(yq, bd_ref, qg_ref[...]))
    ykv = _dot((xr * gkv_ref[...]).astype(BF16), wkv_ref[...])
    kn = _head_norm(ykv[:, :gd], bd_ref, kg_ref[...])
    _store_groups_dup(k_ref, kn)
    _store_groups_t(v_ref, ykv[:, gd:])
    kmean_ref[...] = jnp.mean(kn, axis=0, keepdims=True)


def _moba_project(h2, g_mix, g_kv, w_q, kv_w, q_gain, k_gain, B, S):
    D = h2.shape[1]
    d = N_HEADS * HEAD_DIM
    gd = N_GROUPS * HEAD_DIM
    bd, qg = _qk_side_inputs(q_gain)
    kg = jnp.tile(k_gain.astype(F32), N_GROUPS)[None, :]
    tm = MOBA_BLOCK
    nt = S // tm
    return pl.pallas_call(
        _moba_proj_kernel,
        grid=(B * nt,),
        in_specs=[
            pl.BlockSpec((tm, D), lambda i: (i, 0)),
            pl.BlockSpec((1, D), lambda i: (0, 0)),
            pl.BlockSpec((1, D), lambda i: (0, 0)),
            pl.BlockSpec((D, d), lambda i: (0, 0)),
            pl.BlockSpec((D, 2 * gd), lambda i: (0, 0)),
            pl.BlockSpec((NORM_BLOCK, NORM_BLOCK), lambda i: (0, 0)),
            pl.BlockSpec((1, d), lambda i: (0, 0)),
            pl.BlockSpec((1, gd), lambda i: (0, 0)),
        ],
        out_specs=[
            pl.BlockSpec((None, N_HEADS, LANES, tm), lambda i: (i // nt, 0, 0, i % nt)),
            pl.BlockSpec((None, N_GROUPS, tm, LANES), lambda i: (i // nt, 0, i % nt, 0)),
            pl.BlockSpec((None, N_GROUPS, V_ROWS, tm), lambda i: (i // nt, 0, 0, i % nt)),
            pl.BlockSpec((None, 1, gd), lambda i: (i, 0, 0)),
        ],
        out_shape=[
            jax.ShapeDtypeStruct((B, N_HEADS, LANES, S), BF16),
            jax.ShapeDtypeStruct((B, N_GROUPS, S, LANES), BF16),
            jax.ShapeDtypeStruct((B, N_GROUPS, V_ROWS, S), BF16),
            jax.ShapeDtypeStruct((B * nt, 1, gd), F32),
        ],
        compiler_params=_cparams(1),
    )(h2, g_mix[None, :].astype(F32), g_kv[None, :].astype(F32), w_q.astype(BF16), kv_w.astype(BF16),
      bd, qg, kg)


def _compress_kernel(x_ref, pos_ref, w1_ref, w2_ref, kg_ref, kc_ref, vct_ref, *, n_chunk):
    t = pl.program_id(1)
    half = CMP_BLOCK // 2
    za = [[], []]
    zb = [[], []]
    for l in range(half):
        for p in range(2):
            xl = x_ref[p, pl.ds(l, n_chunk, stride=CMP_STRIDE), :]
            za[p].append((xl + pos_ref[l:l + 1, p * LANES:(p + 1) * LANES]).astype(BF16))
            zb[p].append((xl + pos_ref[half + l:half + l + 1, p * LANES:(p + 1) * LANES]).astype(BF16))
    za = [jnp.concatenate(z, axis=1) for z in za]
    zb = [jnp.concatenate(z, axis=1) for z in zb]
    for g in range(N_GROUPS):
        p, e = g // 2, g % 2
        first = _dot(za[p], w1_ref[0, e])
        second = _dot(zb[p], w1_ref[1, e])
        hid = first + pltpu.roll(second, n_chunk - 1, 0)
        hid = hid * (1.0 / (1.0 + jnp.exp(-hid)))
        out = _dot(hid.astype(BF16), w2_ref[...])

        @pl.when(t == 0)
        def _():
            normed = out * lax.rsqrt(jnp.mean(out * out, axis=-1, keepdims=True) + RMS_EPS) * kg_ref[...]
            kc_ref[g] = normed.astype(kc_ref.dtype)

        @pl.when(t == 1)
        def _():
            vct_ref[g] = out.T[:HEAD_DIM, :].astype(vct_ref.dtype)


def _nsa_compress(cmp_raw, cmp_pos, cmp_w1, cmp_w2, k_gain0, B, S):
    gd = N_GROUPS * HEAD_DIM
    half = CMP_BLOCK // 2
    n_chunk = S // CMP_STRIDE
    pos = jnp.tile(cmp_pos.astype(F32), (1, 1, N_GROUPS))
    w1 = cmp_w1.astype(F32).reshape(2, 2, half, HEAD_DIM, CMP_HIDDEN)
    z = jnp.zeros_like(w1)
    w1p = jnp.stack([jnp.concatenate([w1, z], axis=3), jnp.concatenate([z, w1], axis=3)], axis=2)
    w1p = w1p.reshape(2, 2, 2, half * LANES, CMP_HIDDEN).astype(BF16)
    w2 = jnp.concatenate([cmp_w2, cmp_w2], axis=-1).astype(BF16)
    kg = jnp.tile(k_gain0.astype(F32), 2)[None, :]
    return pl.pallas_call(
        functools.partial(_compress_kernel, n_chunk=n_chunk),
        grid=(B, 2),
        in_specs=[
            pl.BlockSpec((2, S, LANES), lambda b, t: (t, b, 0)),
            pl.BlockSpec((None, CMP_BLOCK, gd), lambda b, t: (t, 0, 0)),
            pl.BlockSpec((None, 2, 2, half * LANES, CMP_HIDDEN), lambda b, t: (t, 0, 0, 0, 0)),
            pl.BlockSpec((None, CMP_HIDDEN, LANES), lambda b, t: (t, 0, 0)),
            pl.BlockSpec((1, LANES), lambda b, t: (0, 0)),
        ],
        out_specs=[
            pl.BlockSpec((None, N_GROUPS, n_chunk, LANES), lambda b, t: (b, 0, 0, 0)),
            pl.BlockSpec((None, N_GROUPS, HEAD_DIM, n_chunk), lambda b, t: (b, 0, 0, 0)),
        ],
        out_shape=[
            jax.ShapeDtypeStruct((B, N_GROUPS, n_chunk, LANES), BF16),
            jax.ShapeDtypeStruct((B, N_GROUPS, HEAD_DIM, n_chunk), BF16),
        ],
        compiler_params=_cparams(2),
    )(cmp_raw, pos, w1p, w2, kg)


_Q_SPEC = pl.BlockSpec((None, R, LANES, QSTEP), lambda b, g, i: (b, g, 0, i))
_GATE_SPEC = pl.BlockSpec((None, None, GATE_ROWS, QSTEP), lambda b, g, i: (b, g, 0, i))
_O_SPEC = pl.BlockSpec((None, R * HEAD_DIM, QSTEP), lambda b, g, i: (b, g, i))


def _chains():
    return [(u, hf) for u in range(NSUB) for hf in range(R // 2)]


def _run_chains(scores, finish, ahead=None, next_scores=None):
    chains = _chains()
    ahead = [scores(ch) for ch in chains[:QK_AHEAD]] if ahead is None else list(ahead)
    for idx, ch in enumerate(chains):
        s = ahead.pop(0)
        nxt = idx + QK_AHEAD
        if nxt < len(chains):
            ahead.append(scores(chains[nxt]))
        elif next_scores is not None:
            ahead.append(next_scores(chains[nxt - len(chains)]))
        finish(ch, s)
    return ahead


def _chain_qt(q_ref, u, hf):
    return jnp.concatenate([q_ref[2 * hf + e, :, u * TQ:(u + 1) * TQ] for e in range(2)], axis=1)


def _store_chain_out(o_ref, ot, gate_ref, gate_col, u, hf):
    for e in range(2):
        r = 2 * hf + e
        o_r = ot[:, e * TQ:(e + 1) * TQ]
        if gate_ref is not None:
            c = 3 * r + gate_col
            o_r = o_r * gate_ref[c:c + 1, u * TQ:(u + 1) * TQ]
        o_ref[r * HEAD_DIM:(r + 1) * HEAD_DIM, u * TQ:(u + 1) * TQ] = o_r.astype(o_ref.dtype)


def _topk_mark_t(score, k):
    row = lax.broadcasted_iota(jnp.int32, score.shape, 0).astype(F32)
    s = score
    for _ in range(k):
        m = jnp.max(s, axis=0, keepdims=True)
        idx = jnp.min(jnp.where(s == m, row, float(LANES)), axis=0, keepdims=True)
        s = jnp.where(row == idx, -jnp.inf, s)
    return s


N_CHAINS = NSUB * (R // 2)
_FLASH_SCRATCH = [pltpu.VMEM((QK_AHEAD, TK, 2 * TQ), BF16), pltpu.VMEM((QK_AHEAD, 8, 2 * TQ), F32),
                  pltpu.VMEM((N_CHAINS, 8, 2 * TQ), F32), pltpu.VMEM((N_CHAINS, V_ROWS, 2 * TQ), F32)]


def _flash_tiles(qas, k_ref, oh_ref, vt_ref, strip_ref, scratch, q0, n_tiles):
    chains = _chains()
    cw = 2 * TQ
    last_k0 = k_ref.shape[0] - TK
    s_scr, mx_scr, m_scr, acc_scr = scratch

    def tile_scores(kt):
        k0 = pl.multiple_of(jnp.minimum(kt * TK, last_k0), TK)
        ka = jnp.concatenate([oh_ref[pl.ds(k0, TK), :], k_ref[pl.ds(k0, TK), :]], axis=1)

        def scores(ch):
            u, hf = ch
            cs = pl.multiple_of(DELTA_MAX - jnp.clip(q0 + u * TQ - k0, 0, DELTA_MAX), LANES)
            s = _dot(ka, qas[ch]).astype(BF16) + strip_ref[pl.ds(cs, TK), hf * cw:(hf + 1) * cw]
            return s, jnp.max(s, axis=0, keepdims=True).astype(F32)

        return scores

    def park(slot, s_and_max):
        s_scr[slot] = s_and_max[0]
        mx_scr[slot] = jnp.broadcast_to(s_and_max[1], mx_scr.shape[1:])

    def one_tile(kt):
        vt = vt_ref[:, pl.ds(pl.multiple_of(kt * TK, TK), TK)]
        cur, nxt = tile_scores(kt), tile_scores(kt + 1)
        in_flight = {}
        for idx, ch in enumerate(chains):
            s, s_max = (s_scr[idx], mx_scr[idx, 0:1]) if idx < QK_AHEAD else in_flight.pop(idx)
            ahead = idx + QK_AHEAD
            if ahead < len(chains):
                in_flight[ahead] = cur(chains[ahead])
            else:
                park(ahead - len(chains), nxt(chains[ahead - len(chains)]))
            m = m_scr[idx, 0:1]
            m_new = jnp.maximum(m, s_max)
            p = jnp.exp2(s - m_new.astype(BF16))
            acc_scr[idx] = jnp.exp2(m - m_new) * acc_scr[idx] + _dot(vt, p)
            m_scr[idx] = jnp.broadcast_to(m_new, m_scr.shape[1:])

    def two_tiles(it, _):
        one_tile(2 * it)
        one_tile(2 * it + 1)
        return 0

    first = tile_scores(0)
    for slot in range(QK_AHEAD):
        park(slot, first(chains[slot]))
    m_scr[...] = jnp.full(m_scr.shape, 3 * NEG, F32)
    acc_scr[...] = jnp.zeros(acc_scr.shape, F32)
    lax.fori_loop(0, n_tiles // 2, two_tiles, 0)
    pl.when(n_tiles % 2 == 1)(lambda: one_tile(n_tiles - 1))
    return {ch: acc_scr[idx, 0:HEAD_DIM] * (1.0 / acc_scr[idx, HEAD_DIM:HEAD_DIM + 1])
            for idx, ch in enumerate(chains)}


def _nsa_cmp_kernel(q_ref, kc_ref, vct_ref, vis_ref, c2st_ref, gate_ref, o_ref, mb_ref, *, n_chunk):
    q0 = pl.program_id(2) * QSTEP
    cw = 2 * TQ
    i_col = lax.broadcasted_iota(jnp.int32, (1, cw), 1) & (TQ - 1)

    def attend(n_used):
        psums = [None] * NSUB

        def scores(ch):
            r0 = pl.multiple_of(n_chunk - (q0 + ch[0] * TQ) // CMP_STRIDE, 8)
            return _dot(kc_ref[0:n_used, :], _chain_qt(q_ref, *ch)) + vis_ref[pl.ds(r0, n_used), :]

        def finish(ch, lcm):
            u, hf = ch
            e = jnp.exp2(lcm - jnp.max(lcm, axis=0, keepdims=True))
            any_visible = q0 + u * TQ + i_col >= CMP_BLOCK - 1
            pc = e * jnp.where(any_visible, 1.0 / jnp.sum(e, axis=0, keepdims=True), 0.0)
            oc = _dot(vct_ref[:, 0:n_used], pc.astype(BF16))
            _store_chain_out(o_ref, oc, gate_ref, 0, u, hf)
            ps = pc[:, :TQ] + pc[:, TQ:]
            psums[u] = ps if psums[u] is None else psums[u] + ps

        _run_chains(scores, finish)
        n_sel = n_used * CMP_STRIDE // SEL_BLOCK
        psum = jnp.concatenate(psums, axis=1)
        imp = None
        for part in _split3(psum):
            term = _dot(c2st_ref[0:n_sel, 0:n_used], part)
            imp = term if imp is None else imp + term
        cur = (q0 + lax.broadcasted_iota(jnp.int32, (n_sel, QSTEP), 1)) // SEL_BLOCK
        j = lax.broadcasted_iota(jnp.int32, (n_sel, QSTEP), 0)
        forced_or_imp = jnp.where(j == 0, -jnp.inf, jnp.where(j >= cur - 1, -jnp.inf, imp))
        score = jnp.where(j <= cur, forced_or_imp, -BIG)
        chosen = _topk_mark_t(score, SEL_TOPN - N_FORCED)
        mb_ref[0:n_sel, :] = jnp.where(chosen == -jnp.inf, 0.0, NEG).astype(mb_ref.dtype)
        if n_sel < LANES:
            mb_ref[n_sel:, :] = jnp.full((LANES - n_sel, QSTEP), NEG, mb_ref.dtype)

    span = LANES * CMP_STRIDE
    for v in range(1, n_chunk // LANES + 1):
        pl.when(q0 // span + 1 == v)(functools.partial(attend, LANES * v))


def _nsa_cmp(qt, kc, vct, gates, B, S):
    n_chunk = S // CMP_STRIDE
    n_sel = S // SEL_BLOCK
    assert n_sel <= LANES and n_chunk % LANES == 0
    cs = jnp.arange(n_chunk)[None, :] * CMP_STRIDE
    ss = jnp.arange(LANES)[:, None] * SEL_BLOCK
    overlap = jnp.clip(jnp.minimum(cs + CMP_BLOCK, ss + SEL_BLOCK) - jnp.maximum(cs, ss), 0, None) / CMP_BLOCK
    n_cmp = (S - CMP_BLOCK) // CMP_STRIDE + 1
    c2st = jnp.where((jnp.arange(n_chunk)[None, :] < n_cmp) & (jnp.arange(LANES)[:, None] < n_sel),
                     overlap, 0.0).astype(BF16)
    c_end = (jnp.arange(2 * n_chunk)[:, None] - n_chunk) * CMP_STRIDE + (CMP_BLOCK - 1)
    vis = jnp.where(c_end <= (jnp.arange(2 * TQ)[None, :] & (TQ - 1)), 0.0, NEG).astype(F32)
    return pl.pallas_call(
        functools.partial(_nsa_cmp_kernel, n_chunk=n_chunk),
        grid=(B, N_GROUPS, S // QSTEP),
        in_specs=[
            _Q_SPEC,
            pl.BlockSpec((None, None, n_chunk, LANES), lambda b, g, i: (b, g, 0, 0)),
            pl.BlockSpec((None, None, HEAD_DIM, n_chunk), lambda b, g, i: (b, g, 0, 0)),
            pl.BlockSpec((2 * n_chunk, 2 * TQ), lambda b, g, i: (0, 0)),
            pl.BlockSpec((LANES, n_chunk), lambda b, g, i: (0, 0)),
            _GATE_SPEC,
        ],
        out_specs=[
            _O_SPEC,
            pl.BlockSpec((None, None, LANES, QSTEP), lambda b, g, i: (b, g, 0, i)),
        ],
        out_shape=[
            jax.ShapeDtypeStruct((B, N_HEADS * HEAD_DIM, S), ATTN_OUT_DTYPE),
            jax.ShapeDtypeStruct((B, N_GROUPS, LANES, S), BF16),
        ],
        compiler_params=_cparams(3),
    )(qt, kc, vct, vis, c2st, gates)


def _nsa_sel_kernel(q_ref, mb_ref, k_ref, oh_ref, vt_ref, strip_ref, gate_ref, o_ref, *scratch):
    q0 = pl.program_id(2) * QSTEP
    qas = {}
    for u, hf in _chains():
        mb = mb_ref[:, u * TQ:(u + 1) * TQ]
        qas[(u, hf)] = jnp.concatenate([jnp.concatenate([mb, mb], axis=1), _chain_qt(q_ref, u, hf)], axis=0)
    n_tiles = (q0 + QSTEP) // TK
    o = _flash_tiles(qas, k_ref, oh_ref, vt_ref, strip_ref, scratch, q0, n_tiles)
    for u, hf in _chains():
        _store_chain_out(o_ref, o[(u, hf)], gate_ref, 1, u, hf)


def _nsa_sel(qt, mb, ksel, vsel_t, strip, gates, B, S):
    onehot = (jnp.arange(S)[:, None] // SEL_BLOCK == jnp.arange(LANES)[None, :]).astype(BF16)
    return pl.pallas_call(
        _nsa_sel_kernel,
        grid=(B, N_GROUPS, S // QSTEP),
        in_specs=[
            _Q_SPEC,
            pl.BlockSpec((None, None, LANES, QSTEP), lambda b, g, i: (b, g, 0, i)),
            pl.BlockSpec((None, None, S, LANES), lambda b, g, i: (b, g, 0, 0)),
            pl.BlockSpec((S, LANES), lambda b, g, i: (0, 0)),
            pl.BlockSpec((None, None, V_ROWS, S), lambda b, g, i: (b, g, 0, 0)),
            pl.BlockSpec((None, STRIP_W, ROWS), lambda b, g, i: (g, 0, 0)),
            _GATE_SPEC,
        ],
        out_specs=_O_SPEC,
        out_shape=jax.ShapeDtypeStruct((B, N_HEADS * HEAD_DIM, S), ATTN_OUT_DTYPE),
        scratch_shapes=_FLASH_SCRATCH,
        compiler_params=_cparams(3),
    )(qt, mb, ksel, onehot, vsel_t, strip, gates)


def _nsa_win_kernel(q_ref, k_ref, vt_ref, strip_ref, gate_ref, o_ref):
    q0 = pl.program_id(2) * QSTEP
    cw = 2 * TQ

    def key_start(u):
        return pl.multiple_of(jnp.maximum(q0 + u * TQ - WINDOW, 0), TQ)

    def scores(ch):
        u, hf = ch
        k0 = key_start(u)
        cs = pl.multiple_of(WINDOW - (q0 + u * TQ - k0), LANES)
        return (_dot(k_ref[pl.ds(k0, WIN_KEYS), :], _chain_qt(q_ref, u, hf)).astype(BF16)
                + strip_ref[pl.ds(cs, WIN_KEYS), hf * cw:(hf + 1) * cw])

    def finish(ch, s):
        u, hf = ch
        e = jnp.exp2(s - jnp.max(s, axis=0, keepdims=True))
        o = _dot(vt_ref[:, pl.ds(key_start(u), WIN_KEYS)], e)
        o = o[:HEAD_DIM] * (1.0 / o[HEAD_DIM:HEAD_DIM + 1])
        _store_chain_out(o_ref, o, gate_ref, 2, u, hf)

    _run_chains(scores, finish)


def _nsa_win(qt, kwin, vwin_t, strip, gates, B, S):
    assert S >= WIN_KEYS
    return pl.pallas_call(
        _nsa_win_kernel,
        grid=(B, N_GROUPS, S // QSTEP),
        in_specs=[
            _Q_SPEC,
            pl.BlockSpec((None, None, S, LANES), lambda b, g, i: (b, g, 0, 0)),
            pl.BlockSpec((None, None, V_ROWS, S), lambda b, g, i: (b, g, 0, 0)),
            pl.BlockSpec((None, WIN_STRIP_W, ROWS), lambda b, g, i: (g, 0, 0)),
            _GATE_SPEC,
        ],
        out_specs=_O_SPEC,
        out_shape=jax.ShapeDtypeStruct((B, N_HEADS * HEAD_DIM, S), ATTN_OUT_DTYPE),
        compiler_params=_cparams(3),
    )(qt, kwin, vwin_t, strip, gates)


def _moba_kernel(q_ref, kmean_ref, k_ref, oh_ref, vt_ref, strip_ref, o_ref, *scratch):
    q0 = pl.program_id(2) * QSTEP
    cw = 2 * TQ
    nb = kmean_ref.shape[0]
    n = lax.broadcasted_iota(jnp.int32, (nb, cw), 0)
    no_block = jnp.full((LANES - nb, cw), NEG, BF16)
    qas = {}
    for u, hf in _chains():
        cblk = (q0 + u * TQ) // MOBA_BLOCK
        qt = _chain_qt(q_ref, u, hf)
        gate = _dot(kmean_ref[...], qt)
        past = n < cblk
        chosen = _topk_mark_t(jnp.where(past, gate, NEG), MOBA_TOPK)
        mb = jnp.where(n == cblk, 0.0, jnp.where(past, jnp.where(chosen == -jnp.inf, 0.0, NEG), NEG)).astype(BF16)
        qas[(u, hf)] = jnp.concatenate([mb, no_block, qt], axis=0)
    n_tiles = (q0 + QSTEP) // TK
    o = _flash_tiles(qas, k_ref, oh_ref, vt_ref, strip_ref, scratch, q0, n_tiles)
    for u, hf in _chains():
        _store_chain_out(o_ref, o[(u, hf)], None, 0, u, hf)


def _moba_attn(qt, kmean, k, vt, strip, B, S):
    n_blk = S // MOBA_BLOCK
    assert n_blk <= LANES
    onehot = (jnp.arange(S)[:, None] // MOBA_BLOCK == jnp.arange(LANES)[None, :]).astype(BF16)
    return pl.pallas_call(
        _moba_kernel,
        grid=(B, N_GROUPS, S // QSTEP),
        in_specs=[
            _Q_SPEC,
            pl.BlockSpec((None, None, kmean.shape[2], LANES), lambda b, g, i: (b, g, 0, 0)),
            pl.BlockSpec((None, None, S, LANES), lambda b, g, i: (b, g, 0, 0)),
            pl.BlockSpec((S, LANES), lambda b, g, i: (0, 0)),
            pl.BlockSpec((None, None, V_ROWS, S), lambda b, g, i: (b, g, 0, 0)),
            pl.BlockSpec((None, STRIP_W, ROWS), lambda b, g, i: (g, 0, 0)),
        ],
        out_specs=_O_SPEC,
        out_shape=jax.ShapeDtypeStruct((B, N_HEADS * HEAD_DIM, S), ATTN_OUT_DTYPE),
        scratch_shapes=_FLASH_SCRATCH,
        compiler_params=_cparams(3),
    )(qt, kmean, k, onehot, vt, strip)


def _out_mlp_kernel(*refs, ff_chunk):
    *o_refs, wo_ref, h_ref, g_ref, wu_ref, wd_ref, out_ref = refs
    o = o_refs[0][...]
    if len(o_refs) > 1:
        o = o.astype(F32)
        for r in o_refs[1:]:
            o = o + r[...].astype(F32)
    h = h_ref[...] + lax.dot_general(o.astype(BF16), wo_ref[...], (((0,), (0,)), ((), ())),
                                     preferred_element_type=F32)
    ms = jnp.mean(h * h, axis=-1, keepdims=True)
    xn = (h * lax.rsqrt(ms + RMS_EPS) * g_ref[...]).astype(BF16)
    acc = h
    for c in range(wu_ref.shape[1] // ff_chunk):
        a = jnp.maximum(_dot(xn, wu_ref[:, c * ff_chunk:(c + 1) * ff_chunk]), 0.0)
        acc = acc + _dot((a * a).astype(BF16), wd_ref[c * ff_chunk:(c + 1) * ff_chunk, :])
    out_ref[...] = acc


def _out_proj_mlp(parts_t, w_out, h2, gain, w_up, w_down, B, S):
    d = parts_t[0].shape[1]
    D = w_out.shape[1]
    F = w_up.shape[1]
    tm = 256
    nt = S // tm
    const = lambda i: (0, 0)
    return pl.pallas_call(
        functools.partial(_out_mlp_kernel, ff_chunk=1024),
        grid=(B * nt,),
        in_specs=[pl.BlockSpec((None, d, tm), lambda i: (i // nt, 0, i % nt)) for _ in parts_t] + [
            pl.BlockSpec((d, D), const),
            pl.BlockSpec((tm, D), lambda i: (i, 0)),
            pl.BlockSpec((1, D), const),
            pl.BlockSpec((D, F), const),
            pl.BlockSpec((F, D), const),
        ],
        out_specs=pl.BlockSpec((tm, D), lambda i: (i, 0)),
        out_shape=jax.ShapeDtypeStruct((B * S, D), F32),
        compiler_params=_cparams(1),
    )(*parts_t, w_out.astype(BF16), h2, gain[None, :].astype(F32), w_up.astype(BF16), w_down.astype(BF16))


def _nsa_mixer(h2, B, S, norm_mix, w_in, q_gain, k_gain, cmp_pos, cmp_w1, cmp_w2, sel_strip, win_strip):
    qt, cmp_raw, ksel, vsel_t, kwin, vwin_t, gates = _nsa_project(h2, norm_mix, w_in, q_gain, k_gain, B, S)
    kc, vct = _nsa_compress(cmp_raw, cmp_pos, cmp_w1, cmp_w2, k_gain[0], B, S)
    o_cmp, mb = _nsa_cmp(qt, kc, vct, gates, B, S)
    o_sel = _nsa_sel(qt, mb, ksel, vsel_t, sel_strip, gates, B, S)
    o_win = _nsa_win(qt, kwin, vwin_t, win_strip, gates, B, S)
    return [o_cmp, o_sel, o_win]


def _shared_kv_and_q(h2, B, S, norm_mix, kv_norm, kv_w, kv_k_gain, w_q, q_gain):
    qt, k, vt, kmean = _moba_project(h2, norm_mix, kv_norm, w_q, kv_w, q_gain, kv_k_gain, B, S)
    n_blk = S // MOBA_BLOCK
    km = kmean.reshape(B, n_blk, N_GROUPS, HEAD_DIM).transpose(0, 2, 1, 3)
    km = jnp.concatenate([km, km], axis=-1)
    km = jnp.pad(km, ((0, 0), (0, 0), (0, -n_blk % 16), (0, 0))).astype(BF16)
    return qt, k, vt, km


def kernel(x, norm_mix, norm_mlp, nsa_w_in, nsa_q_gain, nsa_k_gain, nsa_cmp_pos, nsa_cmp_w1, nsa_cmp_w2,
           nsa_w_out, kv_norm, kv_w, kv_k_gain, moba_w_q, moba_q_gain, moba_w_out, rel_table, mlp_w_up,
           mlp_w_down):
    B, S, D = x.shape
    depth = norm_mix.shape[0]
    n_a = nsa_w_in.shape[0]
    assert S % TK == 0 and S % MOBA_BLOCK == 0
    sel_strip = _bias_strip(rel_table, STRIP_W, DELTA_MAX, 1 << 30)
    win_strip = _bias_strip(rel_table, WIN_STRIP_W, WINDOW, WINDOW)
    h2 = x.reshape(B * S, D)
    shared = None
    for layer in range(depth):
        if layer < n_a:
            i = layer
            parts = _nsa_mixer(h2, B, S, norm_mix[layer], nsa_w_in[i], nsa_q_gain[i], nsa_k_gain[i],
                               nsa_cmp_pos[i], nsa_cmp_w1[i], nsa_cmp_w2[i], sel_strip, win_strip)
            w_out = nsa_w_out[i]
        else:
            j = layer - n_a
            if shared is None:
                qt, k, vt, km = _shared_kv_and_q(h2, B, S, norm_mix[layer], kv_norm, kv_w, kv_k_gain,
                                                 moba_w_q[j], moba_q_gain[j])
                shared = (k, vt, km)
            else:
                qt = _moba_project(h2, norm_mix[layer], kv_norm, moba_w_q[j], kv_w, moba_q_gain[j],
                                   kv_k_gain, B, S)[0]
            k, vt, km = shared
            parts = [_moba_attn(qt, km, k, vt, sel_strip, B, S)]
            w_out = moba_w_out[j]
        h2 = _out_proj_mlp(parts, w_out, h2, norm_mlp[layer], mlp_w_up[layer], mlp_w_down[layer], B, S)
    return h2.reshape(B, S, D)
```

```python
import functools
import math

import jax
import jax.numpy as jnp
from jax import lax
from jax.experimental import pallas as pl
from jax.experimental.pallas import tpu as pltpu

F32 = jnp.float32
BF16 = jnp.bfloat16

N_HEADS = 16
HEAD_DIM = 64
N_GROUPS = 4
R = N_HEADS // N_GROUPS
LANES = 128
CMP_BLOCK = 32
CMP_STRIDE = 16
CMP_HIDDEN = 4 * HEAD_DIM
SEL_BLOCK = 64
SEL_TOPN = 16
N_FORCED = 3
WINDOW = 512
MOBA_BLOCK = 256
MOBA_TOPK = 3
REL_BUCKETS = 32
REL_MAX_DIST = 4096
RMS_EPS = 1e-6
NEG = -1e30
BIG = 1e9
LOG2E = math.log2(math.e)
NORM_BLOCK = 256
GATE_ROWS = 16
ATTN_OUT_DTYPE = BF16
V_ROWS = HEAD_DIM + 16

TQ = 128
QSTEP = 512
NSUB = QSTEP // TQ
TK = 512
ROWS = R * TQ
QK_AHEAD = 3
FAR_DIST = 2897
DELTA_MAX = -(-(FAR_DIST + TK - 1) // LANES) * LANES
STRIP_W = -(-(DELTA_MAX + QSTEP) // 256) * 256
WIN_KEYS = WINDOW + TQ
WIN_STRIP_W = -(-(WINDOW + WIN_KEYS) // 256) * 256
VMEM_LIMIT = 56 * 1024 * 1024


def _cparams(n_axes):
    return pltpu.CompilerParams(dimension_semantics=("arbitrary",) * n_axes,
                                vmem_limit_bytes=VMEM_LIMIT)


def _dot(a, b):
    return jnp.dot(a, b, preferred_element_type=F32)


def _split3(x):
    parts = []
    rem = x
    for _ in range(3):
        hi = rem.astype(BF16)
        parts.append(hi)
        rem = rem - hi.astype(F32)
    return parts


def _strip_kernel(tab_ref, o_ref, *, c0, dlimit, ch):
    g = pl.program_id(0)
    j = pl.program_id(1)
    span = 2 * ch
    shape = (8, span)
    p = lax.broadcasted_iota(jnp.int32, shape, 1)
    d = jnp.where(p < ch, p, p - span) + c0 - j * ch
    n = jnp.maximum(d, 0)
    max_exact = REL_BUCKETS // 2
    nf = jnp.maximum(n, max_exact).astype(F32)
    large = max_exact + (jnp.log(nf / max_exact) / math.log(REL_MAX_DIST / max_exact)
                         * (REL_BUCKETS - max_exact)).astype(jnp.int32)
    large = jnp.minimum(large, REL_BUCKETS - 1)
    bucket = jnp.where(n < max_exact, n, large)
    ok = jnp.where(d >= 0, jnp.where(d < dlimit, 1, 0), 0)
    for r in range(R):
        h = g * R + r
        val = jnp.zeros(shape, F32)
        for b in range(REL_BUCKETS):
            val = jnp.where(bucket == b, tab_ref[b, h], val)
        val = val - tab_ref[REL_BUCKETS - 1, h]
        w = jnp.where(ok > 0, val * LOG2E, NEG)
        wb = jnp.broadcast_to(w[0:1, :], (ch, span))
        o_ref[:, r * TQ:(r + 1) * TQ] = pltpu.roll(wb, 0, 1, stride=1, stride_axis=0)[:, :TQ].astype(o_ref.dtype)


def _bias_strip(rel_table, width, c0, dlimit):
    ch = 256
    return pl.pallas_call(
        functools.partial(_strip_kernel, c0=c0, dlimit=dlimit, ch=ch),
        grid=(N_GROUPS, width // ch),
        in_specs=[pl.BlockSpec(memory_space=pltpu.SMEM)],
        out_specs=pl.BlockSpec((None, ch, ROWS), lambda g, j: (g, j, 0)),
        out_shape=jax.ShapeDtypeStruct((N_GROUPS, width, ROWS), BF16),
        compiler_params=_cparams(2),
    )(rel_table.astype(F32))


def _head_norm(y, bd_ref, gain):
    outs = []
    bw = bd_ref.shape[0]
    for c in range(y.shape[1] // bw):
        yc = y[:, c * bw:(c + 1) * bw]
        ss = _dot((yc * yc).astype(BF16), bd_ref[...])
        outs.append(yc * lax.rsqrt(ss * (1.0 / HEAD_DIM) + RMS_EPS))
    return jnp.concatenate(outs, axis=1) * gain


def _store_heads_t(q_ref, y):
    yt = y.T
    row = lax.broadcasted_iota(jnp.int32, (LANES, y.shape[0]), 0)
    for h in range(N_HEADS):
        pair = yt[(h // 2) * LANES:(h // 2 + 1) * LANES, :]
        keep = row < HEAD_DIM if h % 2 == 0 else row >= HEAD_DIM
        q_ref[h] = jnp.where(keep, pair, 0.0).astype(q_ref.dtype)


def _store_groups_dup(o_ref, y):
    lane = lax.broadcasted_iota(jnp.int32, (y.shape[0], LANES), 1)
    for g in range(N_GROUPS):
        pair = y[:, (g // 2) * LANES:(g // 2 + 1) * LANES]
        keep = lane < HEAD_DIM if g % 2 == 0 else lane >= HEAD_DIM
        m = jnp.where(keep, pair, 0.0)
        o_ref[g] = (m + pltpu.roll(m, HEAD_DIM, 1)).astype(o_ref.dtype)


def _store_groups_t(o_ref, y):
    yt = y.T
    ones = jnp.ones((V_ROWS - HEAD_DIM, y.shape[0]), o_ref.dtype)
    for g in range(N_GROUPS):
        o_ref[g, 0:HEAD_DIM, :] = yt[g * HEAD_DIM:(g + 1) * HEAD_DIM, :].astype(o_ref.dtype)
        o_ref[g, HEAD_DIM:V_ROWS, :] = ones


def _nsa_proj_kernel(x_ref, gn_ref, w_ref, bd_ref, qg_ref, kg_ref,
                     q_ref, cmp_ref, ksel_ref, vsel_ref, kwin_ref, vwin_ref, gate_ref):
    x = x_ref[...]
    ms = jnp.mean(x * x, axis=-1, keepdims=True)
    xn = (x * lax.rsqrt(ms + RMS_EPS) * gn_ref[...]).astype(BF16)
    d = N_HEADS * HEAD_DIM
    gd = N_GROUPS * HEAD_DIM
    yq = _dot(xn, w_ref[:, 0:d])
    _store_heads_t(q_ref, _head_norm(yq, bd_ref, qg_ref[...]))
    ycmp = _dot(xn, w_ref[:, d:d + 2 * gd])
    for c in range(2 * gd // LANES):
        cmp_ref[c] = ycmp[:, c * LANES:(c + 1) * LANES]
    ysel = _dot(xn, w_ref[:, d + 2 * gd:d + 4 * gd])
    _store_groups_dup(ksel_ref, _head_norm(ysel[:, :gd], bd_ref, kg_ref[0:1, :]))
    _store_groups_t(vsel_ref, ysel[:, gd:])
    ywin = _dot(xn, w_ref[:, d + 4 * gd:d + 6 * gd])
    _store_groups_dup(kwin_ref, _head_norm(ywin[:, :gd], bd_ref, kg_ref[1:2, :]))
    _store_groups_t(vwin_ref, ywin[:, gd:])
    yg = _dot(xn, w_ref[:, d + 6 * gd:])
    sgt = (1.0 / (1.0 + jnp.exp(-yg))).T
    for g in range(N_GROUPS):
        gate_ref[g] = sgt[g * LANES:g * LANES + GATE_ROWS, :]


def _qk_side_inputs(q_gain):
    bd = jnp.kron(jnp.eye(NORM_BLOCK // HEAD_DIM, dtype=F32), jnp.ones((HEAD_DIM, HEAD_DIM), F32)).astype(BF16)
    qg = jnp.tile(q_gain.astype(F32) * (HEAD_DIM ** -0.5 * LOG2E), N_HEADS)[None, :]
    return bd, qg


def _nsa_project(x2, gain, w_in, q_gain, k_gain, B, S):
    D = x2.shape[1]
    d = N_HEADS * HEAD_DIM
    gd = N_GROUPS * HEAD_DIM
    wg = w_in[:, d + 6 * gd:].reshape(D, N_GROUPS, 3 * R)
    wg = jnp.pad(wg, ((0, 0), (0, 0), (0, LANES - 3 * R))).reshape(D, N_GROUPS * LANES)
    w = jnp.concatenate([w_in[:, :d + 6 * gd], wg], axis=1).astype(BF16)
    ncol = w.shape[1]
    bd, qg = _qk_side_inputs(q_gain)
    kg = jnp.stack([jnp.tile(k_gain[1].astype(F32), N_GROUPS), jnp.tile(k_gain[2].astype(F32), N_GROUPS)])
    tm = 256
    nt = S // tm
    kspec = pl.BlockSpec((None, N_GROUPS, tm, LANES), lambda i: (i // nt, 0, i % nt, 0))
    vspec = pl.BlockSpec((None, N_GROUPS, V_ROWS, tm), lambda i: (i // nt, 0, 0, i % nt))
    kshape = jax.ShapeDtypeStruct((B, N_GROUPS, S, LANES), BF16)
    vshape = jax.ShapeDtypeStruct((B, N_GROUPS, V_ROWS, S), BF16)
    return pl.pallas_call(
        _nsa_proj_kernel,
        grid=(B * nt,),
        in_specs=[
            pl.BlockSpec((tm, D), lambda i: (i, 0)),
            pl.BlockSpec((1, D), lambda i: (0, 0)),
            pl.BlockSpec((D, ncol), lambda i: (0, 0)),
            pl.BlockSpec((NORM_BLOCK, NORM_BLOCK), lambda i: (0, 0)),
            pl.BlockSpec((1, d), lambda i: (0, 0)),
            pl.BlockSpec((2, gd), lambda i: (0, 0)),
        ],
        out_specs=[
            pl.BlockSpec((None, N_HEADS, LANES, tm), lambda i: (i // nt, 0, 0, i % nt)),
            pl.BlockSpec((2 * gd // LANES, tm, LANES), lambda i: (0, i, 0)),
            kspec, vspec, kspec, vspec,
            pl.BlockSpec((None, N_GROUPS, GATE_ROWS, tm), lambda i: (i // nt, 0, 0, i % nt)),
        ],
        out_shape=[
            jax.ShapeDtypeStruct((B, N_HEADS, LANES, S), BF16),
            jax.ShapeDtypeStruct((2 * gd // LANES, B * S, LANES), F32),
            kshape, vshape, kshape, vshape,
            jax.ShapeDtypeStruct((B, N_GROUPS, GATE_ROWS, S), F32),
        ],
        compiler_params=_cparams(1),
    )(x2, gain[None, :].astype(F32), w, bd, qg, kg)


def _moba_proj_kernel(x_ref, gq_ref, gkv_ref, wq_ref, wkv_ref, bd_ref, qg_ref, kg_ref,
                      q_ref, k_ref, v_ref, kmean_ref):
    x = x_ref[...]
    ms = jnp.mean(x * x, axis=-1, keepdims=True)
    xr = x * lax.rsqrt(ms + RMS_EPS)
    gd = N_GROUPS * HEAD_DIM
    yq = _dot((xr * gq_ref[...]).astype(BF16), wq_ref[...])
    _store_heads_t(q_ref, _head_norm(yq, bd_ref, qg_ref[...]))
    ykv = _dot((xr * gkv_ref[...]).astype(BF16), wkv_ref[...])
    kn = _head_norm(ykv[:, :gd], bd_ref, kg_ref[...])
    _store_groups_dup(k_ref, kn)
    _store_groups_t(v_ref, ykv[:, gd:])
    kmean_ref[...] = jnp.mean(kn, axis=0, keepdims=True)


def _moba_project(h2, g_mix, g_kv, w_q, kv_w, q_gain, k_gain, B, S):
    D = h2.shape[1]
    d = N_HEADS * HEAD_DIM
    gd = N_GROUPS * HEAD_DIM
    bd, qg = _qk_side_inputs(q_gain)
    kg = jnp.tile(k_gain.astype(F32), N_GROUPS)[None, :]
    tm = MOBA_BLOCK
    nt = S // tm
    return pl.pallas_call(
        _moba_proj_kernel,
        grid=(B * nt,),
        in_specs=[
            pl.BlockSpec((tm, D), lambda i: (i, 0)),
            pl.BlockSpec((1, D), lambda i: (0, 0)),
            pl.BlockSpec((1, D), lambda i: (0, 0)),
            pl.BlockSpec((D, d), lambda i: (0, 0)),
            pl.BlockSpec((D, 2 * gd), lambda i: (0, 0)),
            pl.BlockSpec((NORM_BLOCK, NORM_BLOCK), lambda i: (0, 0)),
            pl.BlockSpec((1, d), lambda i: (0, 0)),
            pl.BlockSpec((1, gd), lambda i: (0, 0)),
        ],
        out_specs=[
            pl.BlockSpec((None, N_HEADS, LANES, tm), lambda i: (i // nt, 0, 0, i % nt)),
            pl.BlockSpec((None, N_GROUPS, tm, LANES), lambda i: (i // nt, 0, i % nt, 0)),
            pl.BlockSpec((None, N_GROUPS, V_ROWS, tm), lambda i: (i // nt, 0, 0, i % nt)),
            pl.BlockSpec((None, 1, gd), lambda i: (i, 0, 0)),
        ],
        out_shape=[
            jax.ShapeDtypeStruct((B, N_HEADS, LANES, S), BF16),
            jax.ShapeDtypeStruct((B, N_GROUPS, S, LANES), BF16),
            jax.ShapeDtypeStruct((B, N_GROUPS, V_ROWS, S), BF16),
            jax.ShapeDtypeStruct((B * nt, 1, gd), F32),
        ],
        compiler_params=_cparams(1),
    )(h2, g_mix[None, :].astype(F32), g_kv[None, :].astype(F32), w_q.astype(BF16), kv_w.astype(BF16),
      bd, qg, kg)


def _compress_kernel(x_ref, pos_ref, w1_ref, w2_ref, kg_ref, kc_ref, vct_ref, *, n_chunk):
    t = pl.program_id(1)
    half = CMP_BLOCK // 2
    za = [[], []]
    zb = [[], []]
    for l in range(half):
        for p in range(2):
            xl = x_ref[p, pl.ds(l, n_chunk, stride=CMP_STRIDE), :]
            za[p].append((xl + pos_ref[l:l + 1, p * LANES:(p + 1) * LANES]).astype(BF16))
            zb[p].append((xl + pos_ref[half + l:half + l + 1, p * LANES:(p + 1) * LANES]).astype(BF16))
    za = [jnp.concatenate(z, axis=1) for z in za]
    zb = [jnp.concatenate(z, axis=1) for z in zb]
    for g in range(N_GROUPS):
        p, e = g // 2, g % 2
        first = _dot(za[p], w1_ref[0, e])
        second = _dot(zb[p], w1_ref[1, e])
        hid = first + pltpu.roll(second, n_chunk - 1, 0)
        hid = hid * (1.0 / (1.0 + jnp.exp(-hid)))
        out = _dot(hid.astype(BF16), w2_ref[...])

        @pl.when(t == 0)
        def _():
            normed = out * lax.rsqrt(jnp.mean(out * out, axis=-1, keepdims=True) + RMS_EPS) * kg_ref[...]
            kc_ref[g] = normed.astype(kc_ref.dtype)

        @pl.when(t == 1)
        def _():
            vct_ref[g] = out.T[:HEAD_DIM, :].astype(vct_ref.dtype)


def _nsa_compress(cmp_raw, cmp_pos, cmp_w1, cmp_w2, k_gain0, B, S):
    gd = N_GROUPS * HEAD_DIM
    half = CMP_BLOCK // 2
    n_chunk = S // CMP_STRIDE
    pos = jnp.tile(cmp_pos.astype(F32), (1, 1, N_GROUPS))
    w1 = cmp_w1.astype(F32).reshape(2, 2, half, HEAD_DIM, CMP_HIDDEN)
    z = jnp.zeros_like(w1)
    w1p = jnp.stack([jnp.concatenate([w1, z], axis=3), jnp.concatenate([z, w1], axis=3)], axis=2)
    w1p = w1p.reshape(2, 2, 2, half * LANES, CMP_HIDDEN).astype(BF16)
    w2 = jnp.concatenate([cmp_w2, cmp_w2], axis=-1).astype(BF16)
    kg = jnp.tile(k_gain0.astype(F32), 2)[None, :]
    return pl.pallas_call(
        functools.partial(_compress_kernel, n_chunk=n_chunk),
        grid=(B, 2),
        in_specs=[
            pl.BlockSpec((2, S, LANES), lambda b, t: (t, b, 0)),
            pl.BlockSpec((None, CMP_BLOCK, gd), lambda b, t: (t, 0, 0)),
            pl.BlockSpec((None, 2, 2, half * LANES, CMP_HIDDEN), lambda b, t: (t, 0, 0, 0, 0)),
            pl.BlockSpec((None, CMP_HIDDEN, LANES), lambda b, t: (t, 0, 0)),
            pl.BlockSpec((1, LANES), lambda b, t: (0, 0)),
        ],
        out_specs=[
            pl.BlockSpec((None, N_GROUPS, n_chunk, LANES), lambda b, t: (b, 0, 0, 0)),
            pl.BlockSpec((None, N_GROUPS, HEAD_DIM, n_chunk), lambda b, t: (b, 0, 0, 0)),
        ],
        out_shape=[
            jax.ShapeDtypeStruct((B, N_GROUPS, n_chunk, LANES), BF16),
            jax.ShapeDtypeStruct((B, N_GROUPS, HEAD_DIM, n_chunk), BF16),
        ],
        compiler_params=_cparams(2),
    )(cmp_raw, pos, w1p, w2, kg)


_Q_SPEC = pl.BlockSpec((None, R, LANES, QSTEP), lambda b, g, i: (b, g, 0, i))
_GATE_SPEC = pl.BlockSpec((None, None, GATE_ROWS, QSTEP), lambda b, g, i: (b, g, 0, i))
_O_SPEC = pl.BlockSpec((None, R * HEAD_DIM, QSTEP), lambda b, g, i: (b, g, i))


def _chains():
    return [(u, hf) for u in range(NSUB) for hf in range(R // 2)]


def _run_chains(scores, finish, ahead=None, next_scores=None):
    chains = _chains()
    ahead = [scores(ch) for ch in chains[:QK_AHEAD]] if ahead is None else list(ahead)
    for idx, ch in enumerate(chains):
        s = ahead.pop(0)
        nxt = idx + QK_AHEAD
        if nxt < len(chains):
            ahead.append(scores(chains[nxt]))
        elif next_scores is not None:
            ahead.append(next_scores(chains[nxt - len(chains)]))
        finish(ch, s)
    return ahead


def _chain_qt(q_ref, u, hf):
    return jnp.concatenate([q_ref[2 * hf + e, :, u * TQ:(u + 1) * TQ] for e in range(2)], axis=1)


def _store_chain_out(o_ref, ot, gate_ref, gate_col, u, hf):
    for e in range(2):
        r = 2 * hf + e
        o_r = ot[:, e * TQ:(e + 1) * TQ]
        if gate_ref is not None:
            c = 3 * r + gate_col
            o_r = o_r * gate_ref[c:c + 1, u * TQ:(u + 1) * TQ]
        o_ref[r * HEAD_DIM:(r + 1) * HEAD_DIM, u * TQ:(u + 1) * TQ] = o_r.astype(o_ref.dtype)


def _topk_mark_t(score, k):
    row = lax.broadcasted_iota(jnp.int32, score.shape, 0).astype(F32)
    s = score
    for _ in range(k):
        m = jnp.max(s, axis=0, keepdims=True)
        idx = jnp.min(jnp.where(s == m, row, float(LANES)), axis=0, keepdims=True)
        s = jnp.where(row == idx, -jnp.inf, s)
    return s


N_CHAINS = NSUB * (R // 2)
_FLASH_SCRATCH = [pltpu.VMEM((QK_AHEAD, TK, 2 * TQ), BF16), pltpu.VMEM((QK_AHEAD, 8, 2 * TQ), F32),
                  pltpu.VMEM((N_CHAINS, 8, 2 * TQ), F32), pltpu.VMEM((N_CHAINS, V_ROWS, 2 * TQ), F32)]


def _flash_tiles(qas, k_ref, oh_ref, vt_ref, strip_ref, scratch, q0, n_tiles):
    chains = _chains()
    cw = 2 * TQ
    last_k0 = k_ref.shape[0] - TK
    s_scr, mx_scr, m_scr, acc_scr = scratch

    def tile_scores(kt):
        k0 = pl.multiple_of(jnp.minimum(kt * TK, last_k0), TK)
        ka = jnp.concatenate([oh_ref[pl.ds(k0, TK), :], k_ref[pl.ds(k0, TK), :]], axis=1)

        def scores(ch):
            u, hf = ch
            cs = pl.multiple_of(DELTA_MAX - jnp.clip(q0 + u * TQ - k0, 0, DELTA_MAX), LANES)
            s = _dot(ka, qas[ch]).astype(BF16) + strip_ref[pl.ds(cs, TK), hf * cw:(hf + 1) * cw]
            return s, jnp.max(s, axis=0, keepdims=True).astype(F32)

        return scores

    def park(slot, s_and_max):
        s_scr[slot] = s_and_max[0]
        mx_scr[slot] = jnp.broadcast_to(s_and_max[1], mx_scr.shape[1:])

    def one_tile(kt):
        vt = vt_ref[:, pl.ds(pl.multiple_of(kt * TK, TK), TK)]
        cur, nxt = tile_scores(kt), tile_scores(kt + 1)
        in_flight = {}
        for idx, ch in enumerate(chains):
            s, s_max = (s_scr[idx], mx_scr[idx, 0:1]) if idx < QK_AHEAD else in_flight.pop(idx)
            ahead = idx + QK_AHEAD
            if ahead < len(chains):
                in_flight[ahead] = cur(chains[ahead])
            else:
                park(ahead - len(chains), nxt(chains[ahead - len(chains)]))
            m = m_scr[idx, 0:1]
            m_new = jnp.maximum(m, s_max)
            p = jnp.exp2(s - m_new.astype(BF16))
            acc_scr[idx] = jnp.exp2(m - m_new) * acc_scr[idx] + _dot(vt, p)
            m_scr[idx] = jnp.broadcast_to(m_new, m_scr.shape[1:])

    def two_tiles(it, _):
        one_tile(2 * it)
        one_tile(2 * it + 1)
        return 0

    first = tile_scores(0)
    for slot in range(QK_AHEAD):
        park(slot, first(chains[slot]))
    m_scr[...] = jnp.full(m_scr.shape, 3 * NEG, F32)
    acc_scr[...] = jnp.zeros(acc_scr.shape, F32)
    lax.fori_loop(0, n_tiles // 2, two_tiles, 0)
    pl.when(n_tiles % 2 == 1)(lambda: one_tile(n_tiles - 1))
    return {ch: acc_scr[idx, 0:HEAD_DIM] * (1.0 / acc_scr[idx, HEAD_DIM:HEAD_DIM + 1])
            for idx, ch in enumerate(chains)}


def _nsa_cmp_kernel(q_ref, kc_ref, vct_ref, vis_ref, c2st_ref, gate_ref, o_ref, mb_ref, *, n_chunk):
    q0 = pl.program_id(2) * QSTEP
    cw = 2 * TQ
    i_col = lax.broadcasted_iota(jnp.int32, (1, cw), 1) & (TQ - 1)

    def attend(n_used):
        psums = [None] * NSUB

        def scores(ch):
            r0 = pl.multiple_of(n_chunk - (q0 + ch[0] * TQ) // CMP_STRIDE, 8)
            return _dot(kc_ref[0:n_used, :], _chain_qt(q_ref, *ch)) + vis_ref[pl.ds(r0, n_used), :]

        def finish(ch, lcm):
            u, hf = ch
            e = jnp.exp2(lcm - jnp.max(lcm, axis=0, keepdims=True))
            any_visible = q0 + u * TQ + i_col >= CMP_BLOCK - 1
            pc = e * jnp.where(any_visible, 1.0 / jnp.sum(e, axis=0, keepdims=True), 0.0)
            oc = _dot(vct_ref[:, 0:n_used], pc.astype(BF16))
            _store_chain_out(o_ref, oc, gate_ref, 0, u, hf)
            ps = pc[:, :TQ] + pc[:, TQ:]
            psums[u] = ps if psums[u] is None else psums[u] + ps

        _run_chains(scores, finish)
        n_sel = n_used * CMP_STRIDE // SEL_BLOCK
        psum = jnp.concatenate(psums, axis=1)
        imp = None
        for part in _split3(psum):
            term = _dot(c2st_ref[0:n_sel, 0:n_used], part)
            imp = term if imp is None else imp + term
        cur = (q0 + lax.broadcasted_iota(jnp.int32, (n_sel, QSTEP), 1)) // SEL_BLOCK
        j = lax.broadcasted_iota(jnp.int32, (n_sel, QSTEP), 0)
        forced_or_imp = jnp.where(j == 0, -jnp.inf, jnp.where(j >= cur - 1, -jnp.inf, imp))
        score = jnp.where(j <= cur, forced_or_imp, -BIG)
        chosen = _topk_mark_t(score, SEL_TOPN - N_FORCED)
        mb_ref[0:n_sel, :] = jnp.where(chosen == -jnp.inf, 0.0, NEG).astype(mb_ref.dtype)
        if n_sel < LANES:
            mb_ref[n_sel:, :] = jnp.full((LANES - n_sel, QSTEP), NEG, mb_ref.dtype)

    span = LANES * CMP_STRIDE
    for v in range(1, n_chunk // LANES + 1):
        pl.when(q0 // span + 1 == v)(functools.partial(attend, LANES * v))


def _nsa_cmp(qt, kc, vct, gates, B, S):
    n_chunk = S // CMP_STRIDE
    n_sel = S // SEL_BLOCK
    assert n_sel <= LANES and n_chunk % LANES == 0
    cs = jnp.arange(n_chunk)[None, :] * CMP_STRIDE
    ss = jnp.arange(LANES)[:, None] * SEL_BLOCK
    overlap = jnp.clip(jnp.minimum(cs + CMP_BLOCK, ss + SEL_BLOCK) - jnp.maximum(cs, ss), 0, None) / CMP_BLOCK
    n_cmp = (S - CMP_BLOCK) // CMP_STRIDE + 1
    c2st = jnp.where((jnp.arange(n_chunk)[None, :] < n_cmp) & (jnp.arange(LANES)[:, None] < n_sel),
                     overlap, 0.0).astype(BF16)
    c_end = (jnp.arange(2 * n_chunk)[:, None] - n_chunk) * CMP_STRIDE + (CMP_BLOCK - 1)
    vis = jnp.where(c_end <= (jnp.arange(2 * TQ)[None, :] & (TQ - 1)), 0.0, NEG).astype(F32)
    return pl.pallas_call(
        functools.partial(_nsa_cmp_kernel, n_chunk=n_chunk),
        grid=(B, N_GROUPS, S // QSTEP),
        in_specs=[
            _Q_SPEC,
            pl.BlockSpec((None, None, n_chunk, LANES), lambda b, g, i: (b, g, 0, 0)),
            pl.BlockSpec((None, None, HEAD_DIM, n_chunk), lambda b, g, i: (b, g, 0, 0)),
            pl.BlockSpec((2 * n_chunk, 2 * TQ), lambda b, g, i: (0, 0)),
            pl.BlockSpec((LANES, n_chunk), lambda b, g, i: (0, 0)),
            _GATE_SPEC,
        ],
        out_specs=[
            _O_SPEC,
            pl.BlockSpec((None, None, LANES, QSTEP), lambda b, g, i: (b, g, 0, i)),
        ],
        out_shape=[
            jax.ShapeDtypeStruct((B, N_HEADS * HEAD_DIM, S), ATTN_OUT_DTYPE),
            jax.ShapeDtypeStruct((B, N_GROUPS, LANES, S), BF16),
        ],
        compiler_params=_cparams(3),
    )(qt, kc, vct, vis, c2st, gates)


def _nsa_sel_win_kernel(q_ref, mb_ref, k_ref, oh_ref, vt_ref, strip_ref, kw_ref, vwt_ref, wstrip_ref, gate_ref,
                        o_sel_ref, o_win_ref, *scratch):
    q0 = pl.program_id(2) * QSTEP
    _window_chains(q_ref, kw_ref, vwt_ref, wstrip_ref, gate_ref, o_win_ref, q0)
    qas = {}
    for u, hf in _chains():
        mb = mb_ref[:, u * TQ:(u + 1) * TQ]
        qas[(u, hf)] = jnp.concatenate([jnp.concatenate([mb, mb], axis=1), _chain_qt(q_ref, u, hf)], axis=0)
    n_tiles = (q0 + QSTEP) // TK
    o = _flash_tiles(qas, k_ref, oh_ref, vt_ref, strip_ref, scratch, q0, n_tiles)
    for u, hf in _chains():
        _store_chain_out(o_sel_ref, o[(u, hf)], gate_ref, 1, u, hf)


def _nsa_sel_win(qt, mb, ksel, vsel_t, strip, kwin, vwin_t, win_strip, gates, B, S):
    assert S >= WIN_KEYS
    onehot = (jnp.arange(S)[:, None] // SEL_BLOCK == jnp.arange(LANES)[None, :]).astype(BF16)
    kspec = pl.BlockSpec((None, None, S, LANES), lambda b, g, i: (b, g, 0, 0))
    vspec = pl.BlockSpec((None, None, V_ROWS, S), lambda b, g, i: (b, g, 0, 0))
    oshape = jax.ShapeDtypeStruct((B, N_HEADS * HEAD_DIM, S), ATTN_OUT_DTYPE)
    return pl.pallas_call(
        _nsa_sel_win_kernel,
        grid=(B, N_GROUPS, S // QSTEP),
        in_specs=[
            _Q_SPEC,
            pl.BlockSpec((None, None, LANES, QSTEP), lambda b, g, i: (b, g, 0, i)),
            kspec,
            pl.BlockSpec((S, LANES), lambda b, g, i: (0, 0)),
            vspec,
            pl.BlockSpec((None, STRIP_W, ROWS), lambda b, g, i: (g, 0, 0)),
            kspec,
            vspec,
            pl.BlockSpec((None, WIN_STRIP_W, ROWS), lambda b, g, i: (g, 0, 0)),
            _GATE_SPEC,
        ],
        out_specs=[_O_SPEC, _O_SPEC],
        out_shape=[oshape, oshape],
        scratch_shapes=_FLASH_SCRATCH,
        compiler_params=_cparams(3),
    )(qt, mb, ksel, onehot, vsel_t, strip, kwin, vwin_t, win_strip, gates)


def _window_chains(q_ref, k_ref, vt_ref, strip_ref, gate_ref, o_ref, q0):
    cw = 2 * TQ

    def key_start(u):
        return pl.multiple_of(jnp.maximum(q0 + u * TQ - WINDOW, 0), TQ)

    def scores(ch):
        u, hf = ch
        k0 = key_start(u)
        cs = pl.multiple_of(WINDOW - (q0 + u * TQ - k0), LANES)
        return (_dot(k_ref[pl.ds(k0, WIN_KEYS), :], _chain_qt(q_ref, u, hf)).astype(BF16)
                + strip_ref[pl.ds(cs, WIN_KEYS), hf * cw:(hf + 1) * cw])

    def finish(ch, s):
        u, hf = ch
        e = jnp.exp2(s - jnp.max(s, axis=0, keepdims=True))
        o = _dot(vt_ref[:, pl.ds(key_start(u), WIN_KEYS)], e)
        o = o[:HEAD_DIM] * (1.0 / o[HEAD_DIM:HEAD_DIM + 1])
        _store_chain_out(o_ref, o, gate_ref, 2, u, hf)

    _run_chains(scores, finish)


def _moba_kernel(q_ref, kmean_ref, k_ref, oh_ref, vt_ref, strip_ref, o_ref, *scratch):
    q0 = pl.program_id(2) * QSTEP
    cw = 2 * TQ
    nb = kmean_ref.shape[0]
    n = lax.broadcasted_iota(jnp.int32, (nb, cw), 0)
    no_block = jnp.full((LANES - nb, cw), NEG, BF16)
    qas = {}
    for u, hf in _chains():
        cblk = (q0 + u * TQ) // MOBA_BLOCK
        qt = _chain_qt(q_ref, u, hf)
        gate = _dot(kmean_ref[...], qt)
        past = n < cblk
        chosen = _topk_mark_t(jnp.where(past, gate, NEG), MOBA_TOPK)
        mb = jnp.where(n == cblk, 0.0, jnp.where(past, jnp.where(chosen == -jnp.inf, 0.0, NEG), NEG)).astype(BF16)
        qas[(u, hf)] = jnp.concatenate([mb, no_block, qt], axis=0)
    n_tiles = (q0 + QSTEP) // TK
    o = _flash_tiles(qas, k_ref, oh_ref, vt_ref, strip_ref, scratch, q0, n_tiles)
    for u, hf in _chains():
        _store_chain_out(o_ref, o[(u, hf)], None, 0, u, hf)


def _moba_attn(qt, kmean, k, vt, strip, B, S):
    n_blk = S // MOBA_BLOCK
    assert n_blk <= LANES
    onehot = (jnp.arange(S)[:, None] // MOBA_BLOCK == jnp.arange(LANES)[None, :]).astype(BF16)
    return pl.pallas_call(
        _moba_kernel,
        grid=(B, N_GROUPS, S // QSTEP),
        in_specs=[
            _Q_SPEC,
            pl.BlockSpec((None, None, kmean.shape[2], LANES), lambda b, g, i: (b, g, 0, 0)),
            pl.BlockSpec((None, None, S, LANES), lambda b, g, i: (b, g, 0, 0)),
            pl.BlockSpec((S, LANES), lambda b, g, i: (0, 0)),
            pl.BlockSpec((None, None, V_ROWS, S), lambda b, g, i: (b, g, 0, 0)),
            pl.BlockSpec((None, STRIP_W, ROWS), lambda b, g, i: (g, 0, 0)),
        ],
        out_specs=_O_SPEC,
        out_shape=jax.ShapeDtypeStruct((B, N_HEADS * HEAD_DIM, S), ATTN_OUT_DTYPE),
        scratch_shapes=_FLASH_SCRATCH,
        compiler_params=_cparams(3),
    )(qt, kmean, k, onehot, vt, strip)


def _out_mlp_kernel(*refs, ff_chunk):
    *o_refs, wo_ref, h_ref, g_ref, wu_ref, wd_ref, out_ref = refs
    o = o_refs[0][...]
    if len(o_refs) > 1:
        o = o.astype(F32)
        for r in o_refs[1:]:
            o = o + r[...].astype(F32)
    h = h_ref[...] + lax.dot_general(o.astype(BF16), wo_ref[...], (((0,), (0,)), ((), ())),
                                     preferred_element_type=F32)
    ms = jnp.mean(h * h, axis=-1, keepdims=True)
    xn = (h * lax.rsqrt(ms + RMS_EPS) * g_ref[...]).astype(BF16)
    acc = h
    for c in range(wu_ref.shape[1] // ff_chunk):
        a = jnp.maximum(_dot(xn, wu_ref[:, c * ff_chunk:(c + 1) * ff_chunk]), 0.0)
        acc = acc + _dot((a * a).astype(BF16), wd_ref[c * ff_chunk:(c + 1) * ff_chunk, :])
    out_ref[...] = acc


def _out_proj_mlp(parts_t, w_out, h2, gain, w_up, w_down, B, S):
    d = parts_t[0].shape[1]
    D = w_out.shape[1]
    F = w_up.shape[1]
    tm = 256
    nt = S // tm
    const = lambda i: (0, 0)
    return pl.pallas_call(
        functools.partial(_out_mlp_kernel, ff_chunk=1024),
        grid=(B * nt,),
        in_specs=[pl.BlockSpec((None, d, tm), lambda i: (i // nt, 0, i % nt)) for _ in parts_t] + [
            pl.BlockSpec((d, D), const),
            pl.BlockSpec((tm, D), lambda i: (i, 0)),
            pl.BlockSpec((1, D), const),
            pl.BlockSpec((D, F), const),
            pl.BlockSpec((F, D), const),
        ],
        out_specs=pl.BlockSpec((tm, D), lambda i: (i, 0)),
        out_shape=jax.ShapeDtypeStruct((B * S, D), F32),
        compiler_params=_cparams(1),
    )(*parts_t, w_out.astype(BF16), h2, gain[None, :].astype(F32), w_up.astype(BF16), w_down.astype(BF16))


def _nsa_mixer(h2, B, S, norm_mix, w_in, q_gain, k_gain, cmp_pos, cmp_w1, cmp_w2, sel_strip, win_strip):
    qt, cmp_raw, ksel, vsel_t, kwin, vwin_t, gates = _nsa_project(h2, norm_mix, w_in, q_gain, k_gain, B, S)
    kc, vct = _nsa_compress(cmp_raw, cmp_pos, cmp_w1, cmp_w2, k_gain[0], B, S)
    o_cmp, mb = _nsa_cmp(qt, kc, vct, gates, B, S)
    o_sel, o_win = _nsa_sel_win(qt, mb, ksel, vsel_t, sel_strip, kwin, vwin_t, win_strip, gates, B, S)
    return [o_cmp, o_sel, o_win]


def _shared_kv_and_q(h2, B, S, norm_mix, kv_norm, kv_w, kv_k_gain, w_q, q_gain):
    qt, k, vt, kmean = _moba_project(h2, norm_mix, kv_norm, w_q, kv_w, q_gain, kv_k_gain, B, S)
    n_blk = S // MOBA_BLOCK
    km = kmean.reshape(B, n_blk, N_GROUPS, HEAD_DIM).transpose(0, 2, 1, 3)
    km = jnp.concatenate([km, km], axis=-1)
    km = jnp.pad(km, ((0, 0), (0, 0), (0, -n_blk % 16), (0, 0))).astype(BF16)
    return qt, k, vt, km


def kernel(x, norm_mix, norm_mlp, nsa_w_in, nsa_q_gain, nsa_k_gain, nsa_cmp_pos, nsa_cmp_w1, nsa_cmp_w2,
           nsa_w_out, kv_norm, kv_w, kv_k_gain, moba_w_q, moba_q_gain, moba_w_out, rel_table, mlp_w_up,
           mlp_w_down):
    B, S, D = x.shape
    depth = norm_mix.shape[0]
    n_a = nsa_w_in.shape[0]
    assert S % TK == 0 and S % MOBA_BLOCK == 0
    sel_strip = _bias_strip(rel_table, STRIP_W, DELTA_MAX, 1 << 30)
    win_strip = _bias_strip(rel_table, WIN_STRIP_W, WINDOW, WINDOW)
    h2 = x.reshape(B * S, D)
    shared = None
    for layer in range(depth):
        if layer < n_a:
            i = layer
            parts = _nsa_mixer(h2, B, S, norm_mix[layer], nsa_w_in[i], nsa_q_gain[i], nsa_k_gain[i],
                               nsa_cmp_pos[i], nsa_cmp_w1[i], nsa_cmp_w2[i], sel_strip, win_strip)
            w_out = nsa_w_out[i]
        else:
            j = layer - n_a
            if shared is None:
                qt, k, vt, km = _shared_kv_and_q(h2, B, S, norm_mix[layer], kv_norm, kv_w, kv_k_gain,
                                                 moba_w_q[j], moba_q_gain[j])
                shared = (k, vt, km)
            else:
                qt = _moba_project(h2, norm_mix[layer], kv_norm, moba_w_q[j], kv_w, moba_q_gain[j],
                                   kv_k_gain, B, S)[0]
            k, vt, km = shared
            parts = [_moba_attn(qt, km, k, vt, sel_strip, B, S)]
            w_out = moba_w_out[j]
        h2 = _out_proj_mlp(parts, w_out, h2, norm_mlp[layer], mlp_w_up[layer], mlp_w_down[layer], B, S)
    return h2.reshape(B, S, D)
```

```python
import functools
import math

import jax
import jax.numpy as jnp
from jax import lax
from jax.experimental import pallas as pl
from jax.experimental.pallas import tpu as pltpu

F32 = jnp.float32
BF16 = jnp.bfloat16

N_HEADS = 16
HEAD_DIM = 64
N_GROUPS = 4
R = N_HEADS // N_GROUPS
LANES = 128
CMP_BLOCK = 32
CMP_STRIDE = 16
CMP_HIDDEN = 4 * HEAD_DIM
SEL_BLOCK = 64
SEL_TOPN = 16
N_FORCED = 3
WINDOW = 512
MOBA_BLOCK = 256
MOBA_TOPK = 3
REL_BUCKETS = 32
REL_MAX_DIST = 4096
RMS_EPS = 1e-6
NEG = -1e30
BIG = 1e9
LOG2E = math.log2(math.e)
NORM_BLOCK = 256
GATE_ROWS = 16
ATTN_OUT_DTYPE = BF16
V_ROWS = HEAD_DIM + 16

TQ = 128
QSTEP = 512
NSUB = QSTEP // TQ
TK = 512
ROWS = R * TQ
QK_AHEAD = 4
FAR_DIST = 2897
DELTA_MAX = -(-(FAR_DIST + TK - 1) // LANES) * LANES
STRIP_W = -(-(DELTA_MAX + QSTEP) // 256) * 256
WIN_KEYS = WINDOW + TQ
WIN_STRIP_W = -(-(WINDOW + WIN_KEYS) // 256) * 256
VMEM_LIMIT = 56 * 1024 * 1024


def _cparams(n_axes):
    return pltpu.CompilerParams(dimension_semantics=("arbitrary",) * n_axes,
                                vmem_limit_bytes=VMEM_LIMIT)


def _dot(a, b):
    return jnp.dot(a, b, preferred_element_type=F32)


def _split3(x):
    parts = []
    rem = x
    for _ in range(3):
        hi = rem.astype(BF16)
        parts.append(hi)
        rem = rem - hi.astype(F32)
    return parts


def _strip_kernel(tab_ref, o_ref, *, c0, dlimit, ch):
    g = pl.program_id(0)
    j = pl.program_id(1)
    span = 2 * ch
    shape = (8, span)
    p = lax.broadcasted_iota(jnp.int32, shape, 1)
    d = jnp.where(p < ch, p, p - span) + c0 - j * ch
    n = jnp.maximum(d, 0)
    max_exact = REL_BUCKETS // 2
    nf = jnp.maximum(n, max_exact).astype(F32)
    large = max_exact + (jnp.log(nf / max_exact) / math.log(REL_MAX_DIST / max_exact)
                         * (REL_BUCKETS - max_exact)).astype(jnp.int32)
    large = jnp.minimum(large, REL_BUCKETS - 1)
    bucket = jnp.where(n < max_exact, n, large)
    ok = jnp.where(d >= 0, jnp.where(d < dlimit, 1, 0), 0)
    for r in range(R):
        h = g * R + r
        val = jnp.zeros(shape, F32)
        for b in range(REL_BUCKETS):
            val = jnp.where(bucket == b, tab_ref[b, h], val)
        val = val - tab_ref[REL_BUCKETS - 1, h]
        w = jnp.where(ok > 0, val * LOG2E, NEG)
        wb = jnp.broadcast_to(w[0:1, :], (ch, span))
        o_ref[:, r * TQ:(r + 1) * TQ] = pltpu.roll(wb, 0, 1, stride=1, stride_axis=0)[:, :TQ].astype(o_ref.dtype)


def _bias_strip(rel_table, width, c0, dlimit):
    ch = 256
    return pl.pallas_call(
        functools.partial(_strip_kernel, c0=c0, dlimit=dlimit, ch=ch),
        grid=(N_GROUPS, width // ch),
        in_specs=[pl.BlockSpec(memory_space=pltpu.SMEM)],
        out_specs=pl.BlockSpec((None, ch, ROWS), lambda g, j: (g, j, 0)),
        out_shape=jax.ShapeDtypeStruct((N_GROUPS, width, ROWS), BF16),
        compiler_params=_cparams(2),
    )(rel_table.astype(F32))


def _head_norm(y, bd_ref, gain):
    outs = []
    bw = bd_ref.shape[0]
    for c in range(y.shape[1] // bw):
        yc = y[:, c * bw:(c + 1) * bw]
        ss = _dot((yc * yc).astype(BF16), bd_ref[...])
        outs.append(yc * lax.rsqrt(ss * (1.0 / HEAD_DIM) + RMS_EPS))
    return jnp.concatenate(outs, axis=1) * gain


def _store_heads_t(q_ref, y):
    yt = y.T
    row = lax.broadcasted_iota(jnp.int32, (LANES, y.shape[0]), 0)
    for h in range(N_HEADS):
        pair = yt[(h // 2) * LANES:(h // 2 + 1) * LANES, :]
        keep = row < HEAD_DIM if h % 2 == 0 else row >= HEAD_DIM
        q_ref[h] = jnp.where(keep, pair, 0.0).astype(q_ref.dtype)


def _store_groups_dup(o_ref, y):
    lane = lax.broadcasted_iota(jnp.int32, (y.shape[0], LANES), 1)
    for g in range(N_GROUPS):
        pair = y[:, (g // 2) * LANES:(g // 2 + 1) * LANES]
        keep = lane < HEAD_DIM if g % 2 == 0 else lane >= HEAD_DIM
        m = jnp.where(keep, pair, 0.0)
        o_ref[g] = (m + pltpu.roll(m, HEAD_DIM, 1)).astype(o_ref.dtype)


def _store_groups_t(o_ref, y):
    yt = y.T
    ones = jnp.ones((V_ROWS - HEAD_DIM, y.shape[0]), o_ref.dtype)
    for g in range(N_GROUPS):
        o_ref[g, 0:HEAD_DIM, :] = yt[g * HEAD_DIM:(g + 1) * HEAD_DIM, :].astype(o_ref.dtype)
        o_ref[g, HEAD_DIM:V_ROWS, :] = ones


def _nsa_proj_kernel(x_ref, gn_ref, w_ref, bd_ref, qg_ref, kg_ref,
                     q_ref, cmp_ref, ksel_ref, vsel_ref, kwin_ref, vwin_ref, gate_ref):
    x = x_ref[...]
    ms = jnp.mean(x * x, axis=-1, keepdims=True)
    xn = (x * lax.rsqrt(ms + RMS_EPS) * gn_ref[...]).astype(BF16)
    d = N_HEADS * HEAD_DIM
    gd = N_GROUPS * HEAD_DIM
    yq = _dot(xn, w_ref[:, 0:d])
    _store_heads_t(q_ref, _head_norm(yq, bd_ref, qg_ref[...]))
    ycmp = _dot(xn, w_ref[:, d:d + 2 * gd])
    for c in range(2 * gd // LANES):
        cmp_ref[c] = ycmp[:, c * LANES:(c + 1) * LANES]
    ysel = _dot(xn, w_ref[:, d + 2 * gd:d + 4 * gd])
    _store_groups_dup(ksel_ref, _head_norm(ysel[:, :gd], bd_ref, kg_ref[0:1, :]))
    _store_groups_t(vsel_ref, ysel[:, gd:])
    ywin = _dot(xn, w_ref[:, d + 4 * gd:d + 6 * gd])
    _store_groups_dup(kwin_ref, _head_norm(ywin[:, :gd], bd_ref, kg_ref[1:2, :]))
    _store_groups_t(vwin_ref, ywin[:, gd:])
    yg = _dot(xn, w_ref[:, d + 6 * gd:])
    sgt = (1.0 / (1.0 + jnp.exp(-yg))).T
    for g in range(N_GROUPS):
        gate_ref[g] = sgt[g * LANES:g * LANES + GATE_ROWS, :]


def _qk_side_inputs(q_gain):
    bd = jnp.kron(jnp.eye(NORM_BLOCK // HEAD_DIM, dtype=F32), jnp.ones((HEAD_DIM, HEAD_DIM), F32)).astype(BF16)
    qg = jnp.tile(q_gain.astype(F32) * (HEAD_DIM ** -0.5 * LOG2E), N_HEADS)[None, :]
    return bd, qg


def _nsa_project(x2, gain, w_in, q_gain, k_gain, B, S):
    D = x2.shape[1]
    d = N_HEADS * HEAD_DIM
    gd = N_GROUPS * HEAD_DIM
    wg = w_in[:, d + 6 * gd:].reshape(D, N_GROUPS, 3 * R)
    wg = jnp.pad(wg, ((0, 0), (0, 0), (0, LANES - 3 * R))).reshape(D, N_GROUPS * LANES)
    w = jnp.concatenate([w_in[:, :d + 6 * gd], wg], axis=1).astype(BF16)
    ncol = w.shape[1]
    bd, qg = _qk_side_inputs(q_gain)
    kg = jnp.stack([jnp.tile(k_gain[1].astype(F32), N_GROUPS), jnp.tile(k_gain[2].astype(F32), N_GROUPS)])
    tm = 256
    nt = S // tm
    kspec = pl.BlockSpec((None, N_GROUPS, tm, LANES), lambda i: (i // nt, 0, i % nt, 0))
    vspec = pl.BlockSpec((None, N_GROUPS, V_ROWS, tm), lambda i: (i // nt, 0, 0, i % nt))
    kshape = jax.ShapeDtypeStruct((B, N_GROUPS, S, LANES), BF16)
    vshape = jax.ShapeDtypeStruct((B, N_GROUPS, V_ROWS, S), BF16)
    return pl.pallas_call(
        _nsa_proj_kernel,
        grid=(B * nt,),
        in_specs=[
            pl.BlockSpec((tm, D), lambda i: (i, 0)),
            pl.BlockSpec((1, D), lambda i: (0, 0)),
            pl.BlockSpec((D, ncol), lambda i: (0, 0)),
            pl.BlockSpec((NORM_BLOCK, NORM_BLOCK), lambda i: (0, 0)),
            pl.BlockSpec((1, d), lambda i: (0, 0)),
            pl.BlockSpec((2, gd), lambda i: (0, 0)),
        ],
        out_specs=[
            pl.BlockSpec((None, N_HEADS, LANES, tm), lambda i: (i // nt, 0, 0, i % nt)),
            pl.BlockSpec((2 * gd // LANES, tm, LANES), lambda i: (0, i, 0)),
            kspec, vspec, kspec, vspec,
            pl.BlockSpec((None, N_GROUPS, GATE_ROWS, tm), lambda i: (i // nt, 0, 0, i % nt)),
        ],
        out_shape=[
            jax.ShapeDtypeStruct((B, N_HEADS, LANES, S), BF16),
            jax.ShapeDtypeStruct((2 * gd // LANES, B * S, LANES), F32),
            kshape, vshape, kshape, vshape,
            jax.ShapeDtypeStruct((B, N_GROUPS, GATE_ROWS, S), F32),
        ],
        compiler_params=_cparams(1),
    )(x2, gain[None, :].astype(F32), w, bd, qg, kg)


def _moba_proj_kernel(x_ref, gq_ref, gkv_ref, wq_ref, wkv_ref, bd_ref, qg_ref, kg_ref,
                      q_ref, k_ref, v_ref, kmean_ref):
    x = x_ref[...]
    ms = jnp.mean(x * x, axis=-1, keepdims=True)
    xr = x * lax.rsqrt(ms + RMS_EPS)
    gd = N_GROUPS * HEAD_DIM
    yq = _dot((xr * gq_ref[...]).astype(BF16), wq_ref[...])
    _store_heads_t(q_ref, _head_norm(yq, bd_ref, qg_ref[...]))
    ykv = _dot((xr * gkv_ref[...]).astype(BF16), wkv_ref[...])
    kn = _head_norm(ykv[:, :gd], bd_ref, kg_ref[...])
    _store_groups_dup(k_ref, kn)
    _store_groups_t(v_ref, ykv[:, gd:])
    kmean_ref[...] = jnp.mean(kn, axis=0, keepdims=True)


def _moba_project(h2, g_mix, g_kv, w_q, kv_w, q_gain, k_gain, B, S):
    D = h2.shape[1]
    d = N_HEADS * HEAD_DIM
    gd = N_GROUPS * HEAD_DIM
    bd, qg = _qk_side_inputs(q_gain)
    kg = jnp.tile(k_gain.astype(F32), N_GROUPS)[None, :]
    tm = MOBA_BLOCK
    nt = S // tm
    return pl.pallas_call(
        _moba_proj_kernel,
        grid=(B * nt,),
        in_specs=[
            pl.BlockSpec((tm, D), lambda i: (i, 0)),
            pl.BlockSpec((1, D), lambda i: (0, 0)),
            pl.BlockSpec((1, D), lambda i: (0, 0)),
            pl.BlockSpec((D, d), lambda i: (0, 0)),
            pl.BlockSpec((D, 2 * gd), lambda i: (0, 0)),
            pl.BlockSpec((NORM_BLOCK, NORM_BLOCK), lambda i: (0, 0)),
            pl.BlockSpec((1, d), lambda i: (0, 0)),
            pl.BlockSpec((1, gd), lambda i: (0, 0)),
        ],
        out_specs=[
            pl.BlockSpec((None, N_HEADS, LANES, tm), lambda i: (i // nt, 0, 0, i % nt)),
            pl.BlockSpec((None, N_GROUPS, tm, LANES), lambda i: (i // nt, 0, i % nt, 0)),
            pl.BlockSpec((None, N_GROUPS, V_ROWS, tm), lambda i: (i // nt, 0, 0, i % nt)),
            pl.BlockSpec((None, 1, gd), lambda i: (i, 0, 0)),
        ],
        out_shape=[
            jax.ShapeDtypeStruct((B, N_HEADS, LANES, S), BF16),
            jax.ShapeDtypeStruct((B, N_GROUPS, S, LANES), BF16),
            jax.ShapeDtypeStruct((B, N_GROUPS, V_ROWS, S), BF16),
            jax.ShapeDtypeStruct((B * nt, 1, gd), F32),
        ],
        compiler_params=_cparams(1),
    )(h2, g_mix[None, :].astype(F32), g_kv[None, :].astype(F32), w_q.astype(BF16), kv_w.astype(BF16),
      bd, qg, kg)


def _compress_kernel(x_ref, pos_ref, w1_ref, w2_ref, kg_ref, kc_ref, vct_ref, *, n_chunk):
    t = pl.program_id(1)
    half = CMP_BLOCK // 2
    za = [[], []]
    zb = [[], []]
    for l in range(half):
        for p in range(2):
            xl = x_ref[p, pl.ds(l, n_chunk, stride=CMP_STRIDE), :]
            za[p].append((xl + pos_ref[l:l + 1, p * LANES:(p + 1) * LANES]).astype(BF16))
            zb[p].append((xl + pos_ref[half + l:half + l + 1, p * LANES:(p + 1) * LANES]).astype(BF16))
    za = [jnp.concatenate(z, axis=1) for z in za]
    zb = [jnp.concatenate(z, axis=1) for z in zb]
    for g in range(N_GROUPS):
        p, e = g // 2, g % 2
        first = _dot(za[p], w1_ref[0, e])
        second = _dot(zb[p], w1_ref[1, e])
        hid = first + pltpu.roll(second, n_chunk - 1, 0)
        hid = hid * (1.0 / (1.0 + jnp.exp(-hid)))
        out = _dot(hid.astype(BF16), w2_ref[...])

        @pl.when(t == 0)
        def _():
            normed = out * lax.rsqrt(jnp.mean(out * out, axis=-1, keepdims=True) + RMS_EPS) * kg_ref[...]
            kc_ref[g] = normed.astype(kc_ref.dtype)

        @pl.when(t == 1)
        def _():
            vct_ref[g] = out.T[:HEAD_DIM, :].astype(vct_ref.dtype)


def _nsa_compress(cmp_raw, cmp_pos, cmp_w1, cmp_w2, k_gain0, B, S):
    gd = N_GROUPS * HEAD_DIM
    half = CMP_BLOCK // 2
    n_chunk = S // CMP_STRIDE
    pos = jnp.tile(cmp_pos.astype(F32), (1, 1, N_GROUPS))
    w1 = cmp_w1.astype(F32).reshape(2, 2, half, HEAD_DIM, CMP_HIDDEN)
    z = jnp.zeros_like(w1)
    w1p = jnp.stack([jnp.concatenate([w1, z], axis=3), jnp.concatenate([z, w1], axis=3)], axis=2)
    w1p = w1p.reshape(2, 2, 2, half * LANES, CMP_HIDDEN).astype(BF16)
    w2 = jnp.concatenate([cmp_w2, cmp_w2], axis=-1).astype(BF16)
    kg = jnp.tile(k_gain0.astype(F32), 2)[None, :]
    return pl.pallas_call(
        functools.partial(_compress_kernel, n_chunk=n_chunk),
        grid=(B, 2),
        in_specs=[
            pl.BlockSpec((2, S, LANES), lambda b, t: (t, b, 0)),
            pl.BlockSpec((None, CMP_BLOCK, gd), lambda b, t: (t, 0, 0)),
            pl.BlockSpec((None, 2, 2, half * LANES, CMP_HIDDEN), lambda b, t: (t, 0, 0, 0, 0)),
            pl.BlockSpec((None, CMP_HIDDEN, LANES), lambda b, t: (t, 0, 0)),
            pl.BlockSpec((1, LANES), lambda b, t: (0, 0)),
        ],
        out_specs=[
            pl.BlockSpec((None, N_GROUPS, n_chunk, LANES), lambda b, t: (b, 0, 0, 0)),
            pl.BlockSpec((None, N_GROUPS, HEAD_DIM, n_chunk), lambda b, t: (b, 0, 0, 0)),
        ],
        out_shape=[
            jax.ShapeDtypeStruct((B, N_GROUPS, n_chunk, LANES), BF16),
            jax.ShapeDtypeStruct((B, N_GROUPS, HEAD_DIM, n_chunk), BF16),
        ],
        compiler_params=_cparams(2),
    )(cmp_raw, pos, w1p, w2, kg)


_Q_SPEC = pl.BlockSpec((None, R, LANES, QSTEP), lambda b, g, i: (b, g, 0, i))
_GATE_SPEC = pl.BlockSpec((None, None, GATE_ROWS, QSTEP), lambda b, g, i: (b, g, 0, i))
_O_SPEC = pl.BlockSpec((None, R * HEAD_DIM, QSTEP), lambda b, g, i: (b, g, i))


def _chains():
    return [(u, hf) for u in range(NSUB) for hf in range(R // 2)]


def _run_chains(scores, finish, ahead=None, next_scores=None):
    chains = _chains()
    ahead = [scores(ch) for ch in chains[:QK_AHEAD]] if ahead is None else list(ahead)
    for idx, ch in enumerate(chains):
        s = ahead.pop(0)
        nxt = idx + QK_AHEAD
        if nxt < len(chains):
            ahead.append(scores(chains[nxt]))
        elif next_scores is not None:
            ahead.append(next_scores(chains[nxt - len(chains)]))
        finish(ch, s)
    return ahead


def _chain_qt(q_ref, u, hf):
    return jnp.concatenate([q_ref[2 * hf + e, :, u * TQ:(u + 1) * TQ] for e in range(2)], axis=1)


def _store_chain_out(o_ref, ot, gate_ref, gate_col, u, hf):
    for e in range(2):
        r = 2 * hf + e
        o_r = ot[:, e * TQ:(e + 1) * TQ]
        if gate_ref is not None:
            c = 3 * r + gate_col
            o_r = o_r * gate_ref[c:c + 1, u * TQ:(u + 1) * TQ]
        o_ref[r * HEAD_DIM:(r + 1) * HEAD_DIM, u * TQ:(u + 1) * TQ] = o_r.astype(o_ref.dtype)


def _topk_mark_t(score, k):
    row = lax.broadcasted_iota(jnp.int32, score.shape, 0).astype(F32)
    s = score
    for _ in range(k):
        m = jnp.max(s, axis=0, keepdims=True)
        idx = jnp.min(jnp.where(s == m, row, float(LANES)), axis=0, keepdims=True)
        s = jnp.where(row == idx, -jnp.inf, s)
    return s


N_CHAINS = NSUB * (R // 2)
_FLASH_SCRATCH = [pltpu.VMEM((QK_AHEAD, TK, 2 * TQ), BF16), pltpu.VMEM((QK_AHEAD, 8, 2 * TQ), F32),
                  pltpu.VMEM((N_CHAINS, 8, 2 * TQ), F32), pltpu.VMEM((N_CHAINS, V_ROWS, 2 * TQ), F32)]


def _flash_tiles(qas, k_ref, oh_ref, vt_ref, strip_ref, scratch, q0, n_tiles):
    chains = _chains()
    cw = 2 * TQ
    last_k0 = k_ref.shape[0] - TK
    s_scr, mx_scr, m_scr, acc_scr = scratch

    def tile_scores(kt):
        k0 = pl.multiple_of(jnp.minimum(kt * TK, last_k0), TK)
        ka = jnp.concatenate([oh_ref[pl.ds(k0, TK), :], k_ref[pl.ds(k0, TK), :]], axis=1)

        def scores(ch):
            u, hf = ch
            cs = pl.multiple_of(DELTA_MAX - jnp.clip(q0 + u * TQ - k0, 0, DELTA_MAX), LANES)
            s = _dot(ka, qas[ch]).astype(BF16) + strip_ref[pl.ds(cs, TK), hf * cw:(hf + 1) * cw]
            return s, jnp.max(s, axis=0, keepdims=True).astype(F32)

        return scores

    def park(slot, s_and_max):
        s_scr[slot] = s_and_max[0]
        mx_scr[slot] = jnp.broadcast_to(s_and_max[1], mx_scr.shape[1:])

    def one_tile(kt):
        vt = vt_ref[:, pl.ds(pl.multiple_of(kt * TK, TK), TK)]
        cur, nxt = tile_scores(kt), tile_scores(kt + 1)
        in_flight = {}
        for idx, ch in enumerate(chains):
            s, s_max = (s_scr[idx], mx_scr[idx, 0:1]) if idx < QK_AHEAD else in_flight.pop(idx)
            ahead = idx + QK_AHEAD
            if ahead < len(chains):
                in_flight[ahead] = cur(chains[ahead])
            else:
                park(ahead - len(chains), nxt(chains[ahead - len(chains)]))
            m = m_scr[idx, 0:1]
            m_new = jnp.maximum(m, s_max)
            p = jnp.exp2(s - m_new.astype(BF16))
            acc_scr[idx] = jnp.exp2(m - m_new) * acc_scr[idx] + _dot(vt, p)
            m_scr[idx] = jnp.broadcast_to(m_new, m_scr.shape[1:])

    def two_tiles(it, _):
        one_tile(2 * it)
        one_tile(2 * it + 1)
        return 0

    first = tile_scores(0)
    for slot in range(QK_AHEAD):
        park(slot, first(chains[slot]))
    m_scr[...] = jnp.full(m_scr.shape, 3 * NEG, F32)
    acc_scr[...] = jnp.zeros(acc_scr.shape, F32)
    lax.fori_loop(0, n_tiles // 2, two_tiles, 0)
    pl.when(n_tiles % 2 == 1)(lambda: one_tile(n_tiles - 1))
    return {ch: acc_scr[idx, 0:HEAD_DIM] * (1.0 / acc_scr[idx, HEAD_DIM:HEAD_DIM + 1])
            for idx, ch in enumerate(chains)}


def _nsa_cmp_kernel(q_ref, kc_ref, vct_ref, vis_ref, c2st_ref, gate_ref, o_ref, mb_ref, *, n_chunk):
    q0 = pl.program_id(2) * QSTEP
    cw = 2 * TQ
    i_col = lax.broadcasted_iota(jnp.int32, (1, cw), 1) & (TQ - 1)

    def attend(n_used):
        psums = [None] * NSUB

        def scores(ch):
            r0 = pl.multiple_of(n_chunk - (q0 + ch[0] * TQ) // CMP_STRIDE, 8)
            return _dot(kc_ref[0:n_used, :], _chain_qt(q_ref, *ch)) + vis_ref[pl.ds(r0, n_used), :]

        def finish(ch, lcm):
            u, hf = ch
            e = jnp.exp2(lcm - jnp.max(lcm, axis=0, keepdims=True))
            any_visible = q0 + u * TQ + i_col >= CMP_BLOCK - 1
            pc = e * jnp.where(any_visible, 1.0 / jnp.sum(e, axis=0, keepdims=True), 0.0)
            oc = _dot(vct_ref[:, 0:n_used], pc.astype(BF16))
            _store_chain_out(o_ref, oc, gate_ref, 0, u, hf)
            ps = pc[:, :TQ] + pc[:, TQ:]
            psums[u] = ps if psums[u] is None else psums[u] + ps

        _run_chains(scores, finish)
        n_sel = n_used * CMP_STRIDE // SEL_BLOCK
        psum = jnp.concatenate(psums, axis=1)
        imp = None
        for part in _split3(psum):
            term = _dot(c2st_ref[0:n_sel, 0:n_used], part)
            imp = term if imp is None else imp + term
        cur = (q0 + lax.broadcasted_iota(jnp.int32, (n_sel, QSTEP), 1)) // SEL_BLOCK
        j = lax.broadcasted_iota(jnp.int32, (n_sel, QSTEP), 0)
        forced_or_imp = jnp.where(j == 0, -jnp.inf, jnp.where(j >= cur - 1, -jnp.inf, imp))
        score = jnp.where(j <= cur, forced_or_imp, -BIG)
        chosen = _topk_mark_t(score, SEL_TOPN - N_FORCED)
        mb_ref[0:n_sel, :] = jnp.where(chosen == -jnp.inf, 0.0, NEG).astype(mb_ref.dtype)
        if n_sel < LANES:
            mb_ref[n_sel:, :] = jnp.full((LANES - n_sel, QSTEP), NEG, mb_ref.dtype)

    span = LANES * CMP_STRIDE
    for v in range(1, n_chunk // LANES + 1):
        pl.when(q0 // span + 1 == v)(functools.partial(attend, LANES * v))


def _nsa_cmp(qt, kc, vct, gates, B, S):
    n_chunk = S // CMP_STRIDE
    n_sel = S // SEL_BLOCK
    assert n_sel <= LANES and n_chunk % LANES == 0
    cs = jnp.arange(n_chunk)[None, :] * CMP_STRIDE
    ss = jnp.arange(LANES)[:, None] * SEL_BLOCK
    overlap = jnp.clip(jnp.minimum(cs + CMP_BLOCK, ss + SEL_BLOCK) - jnp.maximum(cs, ss), 0, None) / CMP_BLOCK
    n_cmp = (S - CMP_BLOCK) // CMP_STRIDE + 1
    c2st = jnp.where((jnp.arange(n_chunk)[None, :] < n_cmp) & (jnp.arange(LANES)[:, None] < n_sel),
                     overlap, 0.0).astype(BF16)
    c_end = (jnp.arange(2 * n_chunk)[:, None] - n_chunk) * CMP_STRIDE + (CMP_BLOCK - 1)
    vis = jnp.where(c_end <= (jnp.arange(2 * TQ)[None, :] & (TQ - 1)), 0.0, NEG).astype(F32)
    return pl.pallas_call(
        functools.partial(_nsa_cmp_kernel, n_chunk=n_chunk),
        grid=(B, N_GROUPS, S // QSTEP),
        in_specs=[
            _Q_SPEC,
            pl.BlockSpec((None, None, n_chunk, LANES), lambda b, g, i: (b, g, 0, 0)),
            pl.BlockSpec((None, None, HEAD_DIM, n_chunk), lambda b, g, i: (b, g, 0, 0)),
            pl.BlockSpec((2 * n_chunk, 2 * TQ), lambda b, g, i: (0, 0)),
            pl.BlockSpec((LANES, n_chunk), lambda b, g, i: (0, 0)),
            _GATE_SPEC,
        ],
        out_specs=[
            _O_SPEC,
            pl.BlockSpec((None, None, LANES, QSTEP), lambda b, g, i: (b, g, 0, i)),
        ],
        out_shape=[
            jax.ShapeDtypeStruct((B, N_HEADS * HEAD_DIM, S), ATTN_OUT_DTYPE),
            jax.ShapeDtypeStruct((B, N_GROUPS, LANES, S), BF16),
        ],
        compiler_params=_cparams(3),
    )(qt, kc, vct, vis, c2st, gates)


def _nsa_sel_win_kernel(q_ref, mb_ref, k_ref, oh_ref, vt_ref, strip_ref, kw_ref, vwt_ref, wstrip_ref, gate_ref,
                        o_sel_ref, o_win_ref, *scratch):
    q0 = pl.program_id(2) * QSTEP
    _window_chains(q_ref, kw_ref, vwt_ref, wstrip_ref, gate_ref, o_win_ref, q0)
    qas = {}
    for u, hf in _chains():
        mb = mb_ref[:, u * TQ:(u + 1) * TQ]
        qas[(u, hf)] = jnp.concatenate([jnp.concatenate([mb, mb], axis=1), _chain_qt(q_ref, u, hf)], axis=0)
    n_tiles = (q0 + QSTEP) // TK
    o = _flash_tiles(qas, k_ref, oh_ref, vt_ref, strip_ref, scratch, q0, n_tiles)
    for u, hf in _chains():
        _store_chain_out(o_sel_ref, o[(u, hf)], gate_ref, 1, u, hf)


def _nsa_sel_win(qt, mb, ksel, vsel_t, strip, kwin, vwin_t, win_strip, gates, B, S):
    assert S >= WIN_KEYS
    onehot = (jnp.arange(S)[:, None] // SEL_BLOCK == jnp.arange(LANES)[None, :]).astype(BF16)
    kspec = pl.BlockSpec((None, None, S, LANES), lambda b, g, i: (b, g, 0, 0))
    vspec = pl.BlockSpec((None, None, V_ROWS, S), lambda b, g, i: (b, g, 0, 0))
    oshape = jax.ShapeDtypeStruct((B, N_HEADS * HEAD_DIM, S), ATTN_OUT_DTYPE)
    return pl.pallas_call(
        _nsa_sel_win_kernel,
        grid=(B, N_GROUPS, S // QSTEP),
        in_specs=[
            _Q_SPEC,
            pl.BlockSpec((None, None, LANES, QSTEP), lambda b, g, i: (b, g, 0, i)),
            kspec,
            pl.BlockSpec((S, LANES), lambda b, g, i: (0, 0)),
            vspec,
            pl.BlockSpec((None, STRIP_W, ROWS), lambda b, g, i: (g, 0, 0)),
            kspec,
            vspec,
            pl.BlockSpec((None, WIN_STRIP_W, ROWS), lambda b, g, i: (g, 0, 0)),
            _GATE_SPEC,
        ],
        out_specs=[_O_SPEC, _O_SPEC],
        out_shape=[oshape, oshape],
        scratch_shapes=_FLASH_SCRATCH,
        compiler_params=_cparams(3),
    )(qt, mb, ksel, onehot, vsel_t, strip, kwin, vwin_t, win_strip, gates)


def _window_chains(q_ref, k_ref, vt_ref, strip_ref, gate_ref, o_ref, q0):
    cw = 2 * TQ

    def key_start(u):
        return pl.multiple_of(jnp.maximum(q0 + u * TQ - WINDOW, 0), TQ)

    def scores(ch):
        u, hf = ch
        k0 = key_start(u)
        cs = pl.multiple_of(WINDOW - (q0 + u * TQ - k0), LANES)
        return (_dot(k_ref[pl.ds(k0, WIN_KEYS), :], _chain_qt(q_ref, u, hf)).astype(BF16)
                + strip_ref[pl.ds(cs, WIN_KEYS), hf * cw:(hf + 1) * cw])

    def finish(ch, s):
        u, hf = ch
        e = jnp.exp2(s - jnp.max(s, axis=0, keepdims=True))
        o = _dot(vt_ref[:, pl.ds(key_start(u), WIN_KEYS)], e)
        o = o[:HEAD_DIM] * (1.0 / o[HEAD_DIM:HEAD_DIM + 1])
        _store_chain_out(o_ref, o, gate_ref, 2, u, hf)

    _run_chains(scores, finish)


def _moba_kernel(q_ref, kmean_ref, k_ref, oh_ref, vt_ref, strip_ref, o_ref, *scratch):
    q0 = pl.program_id(2) * QSTEP
    cw = 2 * TQ
    nb = kmean_ref.shape[0]
    n = lax.broadcasted_iota(jnp.int32, (nb, cw), 0)
    no_block = jnp.full((LANES - nb, cw), NEG, BF16)
    qas = {}
    for u, hf in _chains():
        cblk = (q0 + u * TQ) // MOBA_BLOCK
        qt = _chain_qt(q_ref, u, hf)
        gate = _dot(kmean_ref[...], qt)
        past = n < cblk
        chosen = _topk_mark_t(jnp.where(past, gate, NEG), MOBA_TOPK)
        mb = jnp.where(n == cblk, 0.0, jnp.where(past, jnp.where(chosen == -jnp.inf, 0.0, NEG), NEG)).astype(BF16)
        qas[(u, hf)] = jnp.concatenate([mb, no_block, qt], axis=0)
    n_tiles = (q0 + QSTEP) // TK
    o = _flash_tiles(qas, k_ref, oh_ref, vt_ref, strip_ref, scratch, q0, n_tiles)
    for u, hf in _chains():
        _store_chain_out(o_ref, o[(u, hf)], None, 0, u, hf)


def _moba_attn(qt, kmean, k, vt, strip, B, S):
    n_blk = S // MOBA_BLOCK
    assert n_blk <= LANES
    onehot = (jnp.arange(S)[:, None] // MOBA_BLOCK == jnp.arange(LANES)[None, :]).astype(BF16)
    return pl.pallas_call(
        _moba_kernel,
        grid=(B, N_GROUPS, S // QSTEP),
        in_specs=[
            _Q_SPEC,
            pl.BlockSpec((None, None, kmean.shape[2], LANES), lambda b, g, i: (b, g, 0, 0)),
            pl.BlockSpec((None, None, S, LANES), lambda b, g, i: (b, g, 0, 0)),
            pl.BlockSpec((S, LANES), lambda b, g, i: (0, 0)),
            pl.BlockSpec((None, None, V_ROWS, S), lambda b, g, i: (b, g, 0, 0)),
            pl.BlockSpec((None, STRIP_W, ROWS), lambda b, g, i: (g, 0, 0)),
        ],
        out_specs=_O_SPEC,
        out_shape=jax.ShapeDtypeStruct((B, N_HEADS * HEAD_DIM, S), ATTN_OUT_DTYPE),
        scratch_shapes=_FLASH_SCRATCH,
        compiler_params=_cparams(3),
    )(qt, kmean, k, onehot, vt, strip)


def _out_mlp_kernel(*refs, ff_chunk):
    *o_refs, wo_ref, h_ref, g_ref, wu_ref, wd_ref, out_ref = refs
    o = o_refs[0][...]
    if len(o_refs) > 1:
        o = o.astype(F32)
        for r in o_refs[1:]:
            o = o + r[...].astype(F32)
    h = h_ref[...] + lax.dot_general(o.astype(BF16), wo_ref[...], (((0,), (0,)), ((), ())),
                                     preferred_element_type=F32)
    ms = jnp.mean(h * h, axis=-1, keepdims=True)
    xn = (h * lax.rsqrt(ms + RMS_EPS) * g_ref[...]).astype(BF16)
    acc = h
    for c in range(wu_ref.shape[1] // ff_chunk):
        a = jnp.maximum(_dot(xn, wu_ref[:, c * ff_chunk:(c + 1) * ff_chunk]), 0.0)
        acc = acc + _dot((a * a).astype(BF16), wd_ref[c * ff_chunk:(c + 1) * ff_chunk, :])
    out_ref[...] = acc


def _out_proj_mlp(parts_t, w_out, h2, gain, w_up, w_down, B, S):
    d = parts_t[0].shape[1]
    D = w_out.shape[1]
    F = w_up.shape[1]
    tm = 256
    nt = S // tm
    const = lambda i: (0, 0)
    return pl.pallas_call(
        functools.partial(_out_mlp_kernel, ff_chunk=1024),
        grid=(B * nt,),
        in_specs=[pl.BlockSpec((None, d, tm), lambda i: (i // nt, 0, i % nt)) for _ in parts_t] + [
            pl.BlockSpec((d, D), const),
            pl.BlockSpec((tm, D), lambda i: (i, 0)),
            pl.BlockSpec((1, D), const),
            pl.BlockSpec((D, F), const),
            pl.BlockSpec((F, D), const),
        ],
        out_specs=pl.BlockSpec((tm, D), lambda i: (i, 0)),
        out_shape=jax.ShapeDtypeStruct((B * S, D), F32),
        compiler_params=_cparams(1),
    )(*parts_t, w_out.astype(BF16), h2, gain[None, :].astype(F32), w_up.astype(BF16), w_down.astype(BF16))


def _nsa_mixer(h2, B, S, norm_mix, w_in, q_gain, k_gain, cmp_pos, cmp_w1, cmp_w2, sel_strip, win_strip):
    qt, cmp_raw, ksel, vsel_t, kwin, vwin_t, gates = _nsa_project(h2, norm_mix, w_in, q_gain, k_gain, B, S)
    kc, vct = _nsa_compress(cmp_raw, cmp_pos, cmp_w1, cmp_w2, k_gain[0], B, S)
    o_cmp, mb = _nsa_cmp(qt, kc, vct, gates, B, S)
    o_sel, o_win = _nsa_sel_win(qt, mb, ksel, vsel_t, sel_strip, kwin, vwin_t, win_strip, gates, B, S)
    return [o_cmp, o_sel, o_win]


def _shared_kv_and_q(h2, B, S, norm_mix, kv_norm, kv_w, kv_k_gain, w_q, q_gain):
    qt, k, vt, kmean = _moba_project(h2, norm_mix, kv_norm, w_q, kv_w, q_gain, kv_k_gain, B, S)
    n_blk = S // MOBA_BLOCK
    km = kmean.reshape(B, n_blk, N_GROUPS, HEAD_DIM).transpose(0, 2, 1, 3)
    km = jnp.concatenate([km, km], axis=-1)
    km = jnp.pad(km, ((0, 0), (0, 0), (0, -n_blk % 16), (0, 0))).astype(BF16)
    return qt, k, vt, km


def kernel(x, norm_mix, norm_mlp, nsa_w_in, nsa_q_gain, nsa_k_gain, nsa_cmp_pos, nsa_cmp_w1, nsa_cmp_w2,
           nsa_w_out, kv_norm, kv_w, kv_k_gain, moba_w_q, moba_q_gain, moba_w_out, rel_table, mlp_w_up,
           mlp_w_down):
    B, S, D = x.shape
    depth = norm_mix.shape[0]
    n_a = nsa_w_in.shape[0]
    assert S % TK == 0 and S % MOBA_BLOCK == 0
    sel_strip = _bias_strip(rel_table, STRIP_W, DELTA_MAX, 1 << 30)
    win_strip = _bias_strip(rel_table, WIN_STRIP_W, WINDOW, WINDOW)
    h2 = x.reshape(B * S, D)
    shared = None
    for layer in range(depth):
        if layer < n_a:
            i = layer
            parts = _nsa_mixer(h2, B, S, norm_mix[layer], nsa_w_in[i], nsa_q_gain[i], nsa_k_gain[i],
                               nsa_cmp_pos[i], nsa_cmp_w1[i], nsa_cmp_w2[i], sel_strip, win_strip)
            w_out = nsa_w_out[i]
        else:
            j = layer - n_a
            if shared is None:
                qt, k, vt, km = _shared_kv_and_q(h2, B, S, norm_mix[layer], kv_norm, kv_w, kv_k_gain,
                                                 moba_w_q[j], moba_q_gain[j])
                shared = (k, vt, km)
            else:
                qt = _moba_project(h2, norm_mix[layer], kv_norm, moba_w_q[j], kv_w, moba_q_gain[j],
                                   kv_k_gain, B, S)[0]
            k, vt, km = shared
            parts = [_moba_attn(qt, km, k, vt, sel_strip, B, S)]
            w_out = moba_w_out[j]
        h2 = _out_proj_mlp(parts, w_out, h2, norm_mlp[layer], mlp_w_up[layer], mlp_w_down[layer], B, S)
    return h2.reshape(B, S, D)
```
